```python
import jax, jax.numpy as jnp
from jax import lax
import numpy as np

D_MODEL = 2048
BATCH = 2
SEQ = 8192
DEPTH = 1

D_MIX = D_MODEL
RWKV_WIDTH = D_MIX // 2
SB_WIDTH = D_MIX - RWKV_WIDTH
HEAD_DIM = 64
RWKV_HEADS = RWKV_WIDTH // HEAD_DIM
SB_HEADS = SB_WIDTH // HEAD_DIM
DECAY_LORA = 64
ICLR_LORA = 64
PLE_DIM = 256
SB_BLOCK = 128
RMS_EPS = 1e-6
LNX_EPS = 64e-5
L2_EPS = 1e-12
SHIFT_WIDTH = 3 * RWKV_WIDTH + DECAY_LORA + ICLR_LORA
IN_WIDTH = SHIFT_WIDTH + RWKV_WIDTH + 4 * SB_WIDTH

kernel_name = "hymba_rwkv7_stickbreaking_layer"


def rms_norm(x, gain):
    xf = x.astype(jnp.float32)
    y = xf * lax.rsqrt(jnp.mean(xf * xf, axis=-1, keepdims=True) + RMS_EPS)
    return (y * gain.astype(jnp.float32)).astype(x.dtype)


def token_shift(u, mu):
    u_prev = jnp.pad(u, ((0, 0), (1, 0), (0, 0)))[:, :-1]
    return u + (u_prev - u) * mu


def split_heads(t):
    b, s, w = t.shape
    return t.reshape(b, s, w // HEAD_DIM, HEAD_DIM)


def rwkv7_scan(r, decay, k, v, kk, a):
    b, s, h, n = r.shape

    def step(state, inp):
        r_t, w_t, k_t, v_t, kk_t, a_t = inp
        sa = jnp.einsum('bhvk,bhk->bhv', state, kk_t)
        state = (state * w_t[:, :, None, :]
                 - sa[..., None] * (a_t * kk_t)[:, :, None, :]
                 + v_t[..., None] * k_t[:, :, None, :])
        y_t = jnp.einsum('bhvk,bhk->bhv', state, r_t)
        return state, y_t

    xs = (r.transpose(1, 0, 2, 3), decay.transpose(1, 0, 2, 3), k.transpose(1, 0, 2, 3),
          v.transpose(1, 0, 2, 3), kk.transpose(1, 0, 2, 3), a.transpose(1, 0, 2, 3))
    s0 = jnp.zeros((b, h, n, n), jnp.float32)
    _, y = lax.scan(step, s0, xs)
    return y.transpose(1, 0, 2, 3)


def rwkv7_time_mix(u_shift, g_lin, w0, w_up, a0, a_up, k_k, k_a, r_k, lnx_w, lnx_b):
    f32 = jnp.float32
    W = RWKV_WIDTH
    u = u_shift.astype(f32)
    r = u[..., :W]
    k = u[..., W:2 * W]
    v = u[..., 2 * W:3 * W]
    wd = u[..., 3 * W:3 * W + DECAY_LORA]
    ad = u[..., 3 * W + DECAY_LORA:]
    w_log = -jax.nn.softplus(-(w0.astype(f32) + jnp.tanh(wd) @ w_up.astype(f32))) - 0.5
    decay = jnp.exp(-jnp.exp(w_log))
    a = jax.nn.sigmoid(a0.astype(f32) + ad @ a_up.astype(f32))
    kk = split_heads(k * k_k.astype(f32))
    kk = kk * lax.rsqrt(jnp.sum(kk * kk, axis=-1, keepdims=True) + L2_EPS)
    k = k * (1.0 + (a - 1.0) * k_a.astype(f32))
    rh, kh, vh = split_heads(r), split_heads(k), split_heads(v)
    y = rwkv7_scan(rh, split_heads(decay), kh, vh, kk, split_heads(a))
    mean = jnp.mean(y, axis=-1, keepdims=True)
    var = jnp.mean(jnp.square(y - mean), axis=-1, keepdims=True)
    y = (y - mean) * lax.rsqrt(var + LNX_EPS)
    b, s = y.shape[:2]
    y = y.reshape(b, s, W) * lnx_w.astype(f32) + lnx_b.astype(f32)
    bonus = jnp.sum(rh * kh * split_heads(jnp.broadcast_to(r_k.astype(f32), r.shape)), axis=-1, keepdims=True) * vh
    y = y + bonus.reshape(b, s, W)
    return y * jax.nn.silu(g_lin.astype(f32))


def stick_breaking_attention(q, k, v):
    f32 = jnp.float32
    scale = HEAD_DIM ** -0.5
    qh = q.astype(f32).transpose(0, 2, 1, 3)
    kh = k.astype(f32).transpose(0, 2, 1, 3)
    vh = v.astype(f32).transpose(0, 2, 1, 3)
    s_len = qh.shape[2]
    outs = []
    for blk in range(s_len // SB_BLOCK):
        t0 = blk * SB_BLOCK
        end = t0 + SB_BLOCK
        qb = qh[:, :, t0:end]
        kb = kh[:, :, :end]
        vb = vh[:, :, :end]
        z = jnp.einsum('bhqd,bhkd->bhqk', qb, kb) * scale
        t_pos = t0 + jnp.arange(SB_BLOCK)[:, None]
        s_pos = jnp.arange(end)[None, :]
        strict = s_pos < t_pos
        log_beta = jax.nn.log_sigmoid(z)
        log_1mb = jnp.where(strict, jax.nn.log_sigmoid(-z), 0.0)
        suffix = lax.cumsum(log_1mb, axis=3, reverse=True) - log_1mb
        attn = jnp.where(strict, jnp.exp(log_beta + suffix), 0.0)
        outs.append(jnp.einsum('bhqk,bhkd->bhqd', attn, vb))
    o = jnp.concatenate(outs, axis=2)
    b = o.shape[0]
    return o.transpose(0, 2, 1, 3).reshape(b, s_len, SB_HEADS * HEAD_DIM)


def setup_inputs(seed: int = 0) -> dict:
    key = jax.random.key(seed)
    ks = jax.random.split(key, 20)
    f32 = jnp.float32
    nrm = lambda k, shape, s: jax.random.normal(k, shape, f32) * s
    return {
        "x": nrm(ks[0], (BATCH, SEQ, D_MODEL), 1.0),
        "p": nrm(ks[1], (DEPTH, BATCH, SEQ, PLE_DIM), 1.0),
        "g_pre": 1.0 + nrm(ks[2], (DEPTH, D_MODEL), 0.1),
        "w_in": nrm(ks[3], (DEPTH, D_MODEL, IN_WIDTH), D_MODEL ** -0.5),
        "mu_shift": jax.random.uniform(ks[4], (DEPTH, SHIFT_WIDTH), f32),
        "w0": jax.random.uniform(ks[5], (DEPTH, RWKV_WIDTH), f32, -4.0, 0.0),
        "w_up": nrm(ks[6], (DEPTH, DECAY_LORA, RWKV_WIDTH), 0.5 * DECAY_LORA ** -0.5),
        "a0": nrm(ks[7], (DEPTH, RWKV_WIDTH), 0.1),
        "a_up": nrm(ks[8], (DEPTH, ICLR_LORA, RWKV_WIDTH), 0.5 * ICLR_LORA ** -0.5),
        "k_k": 0.85 + nrm(ks[9], (DEPTH, RWKV_WIDTH), 0.05),
        "k_a": 1.0 + nrm(ks[10], (DEPTH, RWKV_WIDTH), 0.05),
        "r_k": nrm(ks[11], (DEPTH, RWKV_WIDTH), 0.1),
        "lnx_w": 1.0 + nrm(ks[12], (DEPTH, RWKV_WIDTH), 0.1),
        "lnx_b": nrm(ks[13], (DEPTH, RWKV_WIDTH), 0.02),
        "w_out": nrm(ks[14], (DEPTH, D_MIX, D_MODEL), D_MIX ** -0.5),
        "g_post": 1.0 + nrm(ks[15], (DEPTH, D_MODEL), 0.1),
        "w_ple": nrm(ks[16], (DEPTH, PLE_DIM, D_MODEL), PLE_DIM ** -0.5),
        "w_plg": nrm(ks[17], (DEPTH, D_MODEL, D_MODEL), D_MODEL ** -0.5),
        "g_ple": 1.0 + nrm(ks[18], (DEPTH, D_MODEL), 0.1),
    }


def reference(x, p, g_pre, w_in, mu_shift, w0, w_up, a0, a_up, k_k, k_a, r_k,
              lnx_w, lnx_b, w_out, g_post, w_ple, w_plg, g_ple):
    h = x
    for i in range(DEPTH):
        xn = rms_norm(h, g_pre[i])
        u = xn @ w_in[i]
        u_shift = token_shift(u[..., :SHIFT_WIDTH], mu_shift[i])
        o = SHIFT_WIDTH
        g_rwkv = u[..., o:o + RWKV_WIDTH]; o += RWKV_WIDTH
        q_sb = u[..., o:o + SB_WIDTH]; o += SB_WIDTH
        k_sb = u[..., o:o + SB_WIDTH]; o += SB_WIDTH
        v_sb = u[..., o:o + SB_WIDTH]; o += SB_WIDTH
        g_sb = u[..., o:o + SB_WIDTH]
        y_a = rwkv7_time_mix(u_shift, g_rwkv, w0[i], w_up[i], a0[i], a_up[i],
                             k_k[i], k_a[i], r_k[i], lnx_w[i], lnx_b[i])
        y_b = stick_breaking_attention(split_heads(q_sb), split_heads(k_sb), split_heads(v_sb))
        y_b = y_b * jax.nn.silu(g_sb.astype(jnp.float32))
        mix = jnp.concatenate([y_a, y_b], axis=-1).astype(h.dtype) @ w_out[i]
        h = h + rms_norm(mix, g_post[i])
        ple = rms_norm(p[i] @ w_ple[i], g_ple[i])
        h = h + ple * jax.nn.sigmoid(h @ w_plg[i])
    return h
```

```python
import functools

import jax
import jax.numpy as jnp
from jax import lax
from jax.experimental import pallas as pl
from jax.experimental.pallas import tpu as pltpu

F32 = jnp.float32
BF16 = jnp.bfloat16

HEAD_DIM = 64
LANES = 128
RMS_EPS = 1e-6
LNX_EPS = 64e-5
L2_EPS = 1e-12
CHUNK = 64
SB_TQ = 256
SB_TK = 128
VMEM_LIMIT = 56 * 1024 * 1024


def _dot(a, b):
    return jnp.dot(a, b, preferred_element_type=F32)


def _dot_nt(a, b):
    return lax.dot_general(a, b, (((1,), (1,)), ((), ())), preferred_element_type=F32)


def _dot_tn(a, b):
    return lax.dot_general(a, b, (((0,), (0,)), ((), ())), preferred_element_type=F32)


def _split2(x):
    hi = x.astype(BF16)
    lo = (x - hi.astype(F32)).astype(BF16)
    return hi, lo


def _mm(a, b):
    return _dot(a.astype(BF16), b.astype(BF16))


def _mm_nt(a, b):
    return _dot_nt(a.astype(BF16), b.astype(BF16))


def _mm_tn(a, b):
    return _dot_tn(a.astype(BF16), b.astype(BF16))


def _sigmoid(x):
    return 1.0 / (1.0 + jnp.exp(-x))


def _norm_matmul_kernel(x_ref, g_ref, w_ref, o_ref, xn_ref):
    @pl.when(pl.program_id(1) == 0)
    def _():
        x = x_ref[...]
        ms = jnp.mean(x * x, axis=-1, keepdims=True)
        xn_ref[...] = (x * lax.rsqrt(ms + RMS_EPS) * g_ref[...]).astype(BF16)

    o_ref[...] = _dot(xn_ref[...], w_ref[...])


def _norm_matmul(x2, gain, w, tm, tn):
    m, k = x2.shape
    n = w.shape[1]
    return pl.pallas_call(
        _norm_matmul_kernel,
        grid=(m // tm, n // tn),
        in_specs=[
            pl.BlockSpec((tm, k), lambda i, j: (i, 0)),
            pl.BlockSpec((1, k), lambda i, j: (0, 0)),
            pl.BlockSpec((k, tn), lambda i, j: (0, j)),
        ],
        out_specs=pl.BlockSpec((tm, tn), lambda i, j: (i, j)),
        out_shape=jax.ShapeDtypeStruct((m, n), F32),
        scratch_shapes=[pltpu.VMEM((tm, k), BF16)],
        compiler_params=pltpu.CompilerParams(
            dimension_semantics=("parallel", "arbitrary"),
            vmem_limit_bytes=VMEM_LIMIT),
        name="norm_matmul",
    )(x2, gain.reshape(1, k), w)


def _rwkv_kernel(u_ref, g_ref, mu_ref, w0_ref, wup_ref, a0_ref, aup_ref, kk_ref, ka_ref,
                 rk_ref, lw_ref, lb_ref, o_ref, prev_ref, st_ref, *, width):
    L = CHUNK
    n_slab = width // LANES

    @pl.when(pl.program_id(1) == 0)
    def _():
        prev_ref[...] = jnp.zeros_like(prev_ref)
        st_ref[...] = jnp.zeros_like(st_ref)

    u = u_ref[0]
    row = lax.broadcasted_iota(jnp.int32, u.shape, 0)
    u_prev = jnp.where(row == 0, prev_ref[0:1, :], pltpu.roll(u, 1, 0))
    prev_ref[0:1, :] = u[L - 1:L, :]
    us = u + (u_prev - u) * mu_ref[...]

    lane = lax.broadcasted_iota(jnp.int32, (L, LANES), 1)
    m0 = lane < HEAD_DIM
    xl = us[:, 3 * width:3 * width + LANES]
    xl = jnp.where(m0, jnp.tanh(xl), xl).astype(BF16)
    lora_w = _dot(xl, wup_ref[...])
    lora_a = _dot(xl, aup_ref[...])

    r2 = lax.broadcasted_iota(jnp.int32, (2 * L, 2 * L), 0)
    c2 = lax.broadcasted_iota(jnp.int32, (2 * L, 2 * L), 1)
    same = (r2 >= L) == (c2 >= L)
    strict_m = same & (r2 > c2)
    incl_m = same & (r2 >= c2)
    eye = r2 == c2
    eye_f = jnp.where(eye, 1.0, 0.0).astype(F32)
    block_ones = jnp.where(same, 1.0, 0.0).astype(BF16)
    rl = lax.broadcasted_iota(jnp.int32, (L, L), 0)
    cl = lax.broadcasted_iota(jnp.int32, (L, L), 1)
    lower_incl = jnp.where(rl >= cl, 1.0, 0.0).astype(BF16)

    def head_sum(x):
        hi, lo = _split2(x)
        return _dot(hi, block_ones) + _dot(lo, block_ones)

    def stack(x):
        return jnp.concatenate([jnp.where(m0, x, 0.0), jnp.where(m0, 0.0, x)], axis=0)

    for s in range(n_slab):
        sl = slice(s * LANES, (s + 1) * LANES)
        r = us[:, s * LANES:(s + 1) * LANES]
        k = us[:, width + s * LANES:width + (s + 1) * LANES]
        v = us[:, 2 * width + s * LANES:2 * width + (s + 1) * LANES]

        wpre = w0_ref[:, sl] + lora_w[:, sl]
        softplus = jnp.maximum(-wpre, 0.0) + jnp.log(1.0 + jnp.exp(-jnp.abs(wpre)))
        logw = -jnp.exp(-softplus - 0.5)
        a = _sigmoid(a0_ref[:, sl] + lora_a[:, sl])
        kkr = k * kk_ref[:, sl]
        kk = kkr * lax.rsqrt(head_sum(kkr * kkr) + L2_EPS)
        k2 = k * (1.0 + (a - 1.0) * ka_ref[:, sl])
        b = a * kk

        lw_hi, lw_lo = _split2(logw)
        cum = _dot(lower_incl, lw_hi) + _dot(lower_incl, lw_lo)
        p_in = jnp.exp(cum)
        p_ex = jnp.exp(cum - logw)
        q_in = jnp.exp(-cum)
        p_last = p_in[L - 1:L, :]
        at = kk * p_ex
        kt = k2 * q_in
        bt = b * q_in
        rt = r * p_in

        a_s, k_s, b_s, r_s, v_s = stack(at), stack(kt), stack(bt), stack(rt), stack(v)
        kh_s, bh_s = stack(kt * p_last), stack(bt * p_last)

        akk = jnp.where(strict_m, _mm_nt(a_s, k_s), 0.0)
        aab = jnp.where(strict_m, _mm_nt(a_s, b_s), 0.0)
        ark = jnp.where(incl_m, _mm_nt(r_s, k_s), 0.0)
        arb = jnp.where(incl_m, _mm_nt(r_s, b_s), 0.0)

        tinv = eye_f - aab
        pw = _mm(aab, aab)
        n_sq = 1
        while True:
            tinv = tinv + _mm(tinv, pw)
            n_sq *= 2
            if 2 * n_sq >= L:
                break
            pw = _mm(pw, pw)

        akk_v = _mm(akk, v_s)
        ark_v = _mm(ark, v_s)
        a_hat = _mm(tinv, a_s)
        u_bar = _mm(tinv, akk_v)
        q_s = r_s - _mm(arb, a_hat)
        y0 = ark_v - _mm(arb, u_bar)
        g_mat = jnp.where(eye, p_last, 0.0) - _mm_tn(bh_s, a_hat)
        h_mat = _mm_tn(kh_s, v_s) - _mm_tn(bh_s, u_bar)

        state = st_ref[s]
        y_s = _mm(q_s, state) + y0
        st_ref[s] = _mm(g_mat, state) + h_mat
        y = y_s[0:L, :] + y_s[L:2 * L, :]

        mean = head_sum(y) * (1.0 / HEAD_DIM)
        yc = y - mean
        var = head_sum(yc * yc) * (1.0 / HEAD_DIM)
        yn = yc * lax.rsqrt(var + LNX_EPS) * lw_ref[:, sl] + lb_ref[:, sl]
        bonus = head_sum(r * k2 * rk_ref[:, sl]) * v
        g = g_ref[0, :, sl]
        o_ref[0, :, sl] = ((yn + bonus) * (g * _sigmoid(g))).astype(o_ref.dtype)


def _rwkv(u1, u2, mu, w0, wup_p, a0, aup_p, k_k, k_a, r_k, lnx_w, lnx_b, width):
    b, s, sw = u1.shape
    n_slab = width // LANES
    row = lambda a: a.reshape(1, -1)
    vec = pl.BlockSpec((1, width), lambda i, j: (0, 0))
    return pl.pallas_call(
        functools.partial(_rwkv_kernel, width=width),
        grid=(b, s // CHUNK),
        in_specs=[
            pl.BlockSpec((1, CHUNK, sw), lambda i, j: (i, j, 0)),
            pl.BlockSpec((1, CHUNK, width), lambda i, j: (i, j, 0)),
            pl.BlockSpec((1, sw), lambda i, j: (0, 0)),
            vec,
            pl.BlockSpec((LANES, width), lambda i, j: (0, 0)),
            vec,
            pl.BlockSpec((LANES, width), lambda i, j: (0, 0)),
            vec, vec, vec, vec, vec,
        ],
        out_specs=pl.BlockSpec((1, CHUNK, width), lambda i, j: (i, j, 0)),
        out_shape=jax.ShapeDtypeStruct((b, s, width), BF16),
        scratch_shapes=[pltpu.VMEM((8, sw), F32),
                        pltpu.VMEM((n_slab, LANES, LANES), F32)],
        compiler_params=pltpu.CompilerParams(
            dimension_semantics=("parallel", "arbitrary"),
            vmem_limit_bytes=VMEM_LIMIT),
        name="rwkv7_chunked",
    )(u1, u2, row(mu), row(w0), wup_p, row(a0), aup_p, row(k_k), row(k_a), row(r_k),
      row(lnx_w), row(lnx_b))


def _stickbreak_kernel(qt_ref, k_ref, vt_ref, o_ref):
    i = pl.program_id(2)
    q = qt_ref[0, 0]
    nblk = (i + 1) * (SB_TQ // SB_TK)
    rows = lax.broadcasted_iota(jnp.int32, (SB_TK, SB_TQ), 0)
    cols = lax.broadcasted_iota(jnp.int32, (SB_TK, SB_TQ), 1)
    ur = lax.broadcasted_iota(jnp.int32, (SB_TK, SB_TK), 0)
    uc = lax.broadcasted_iota(jnp.int32, (SB_TK, SB_TK), 1)
    upper = jnp.where(uc > ur, 1.0, 0.0).astype(BF16)

    def body(jj, carry_acc):
        carry, acc = carry_acc
        j = nblk - 1 - jj
        k0 = pl.multiple_of(j * SB_TK, SB_TK)
        kb = k_ref[0, 0, pl.ds(k0, SB_TK), :]
        z = _dot(kb, q)
        strict = (k0 + rows) < (i * SB_TQ + cols)
        soft = jnp.log(1.0 + jnp.exp(-jnp.abs(z)))
        log_beta = jnp.minimum(z, 0.0) - soft
        log_1mb = jnp.where(strict, log_beta - z, 0.0)
        hi, lo = _split2(log_1mb)
        suffix = _dot(upper, hi) + _dot(upper, lo)
        total = suffix[0:1, :] + log_1mb[0:1, :]
        pr = jnp.where(strict, jnp.exp(log_beta + suffix), 0.0).astype(BF16)
        vb = vt_ref[0, 0, :, pl.ds(k0, SB_TK)]
        acc = acc + jnp.exp(carry) * _dot(vb, pr)
        return carry + total, acc

    init = (jnp.zeros((1, SB_TQ), F32), jnp.zeros((HEAD_DIM, SB_TQ), F32))
    _, acc = lax.fori_loop(0, nblk, body, init)
    o_ref[0, 0] = acc


def _stickbreak(qt, kh, vt):
    b, h, d, s = qt.shape
    return pl.pallas_call(
        _stickbreak_kernel,
        grid=(b, h, s // SB_TQ),
        in_specs=[
            pl.BlockSpec((1, 1, d, SB_TQ), lambda bi, hi, qi: (bi, hi, 0, qi)),
            pl.BlockSpec((1, 1, s, d), lambda bi, hi, qi: (bi, hi, 0, 0)),
            pl.BlockSpec((1, 1, d, s), lambda bi, hi, qi: (bi, hi, 0, 0)),
        ],
        out_specs=pl.BlockSpec((1, 1, d, SB_TQ), lambda bi, hi, qi: (bi, hi, 0, qi)),
        out_shape=jax.ShapeDtypeStruct((b, h, d, s), F32),
        compiler_params=pltpu.CompilerParams(
            dimension_semantics=("parallel", "parallel", "arbitrary"),
            vmem_limit_bytes=VMEM_LIMIT),
        name="stickbreak_attn",
    )(qt, kh, vt)


def _final_kernel(x_ref, ya_ref, yb_ref, gsb_ref, p_ref, wout_ref, gpost_ref, wple_ref,
                  gple_ref, wplg_ref, o_ref, *, width):
    def rms(t, gain):
        return t * lax.rsqrt(jnp.mean(t * t, axis=-1, keepdims=True) + RMS_EPS) * gain

    gs = gsb_ref[...]
    yb = (yb_ref[...] * (gs * _sigmoid(gs))).astype(BF16)
    mix = _dot(ya_ref[...], wout_ref[0:width, :]) + _dot(yb, wout_ref[width:, :])
    h = x_ref[...] + rms(mix, gpost_ref[...])
    ple = rms(_dot(p_ref[...].astype(BF16), wple_ref[...]), gple_ref[...])
    gate = _sigmoid(_dot(h.astype(BF16), wplg_ref[...]))
    o_ref[...] = h + ple * gate


def _final(x2, ya, yb, u2, p2, w_out, g_post, w_ple, g_ple, w_plg, width, gsb_block, tm):
    m, d = x2.shape
    pd = p2.shape[1]
    const = lambda shape: pl.BlockSpec(shape, lambda i: (0, 0), pipeline_mode=pl.Buffered(1))
    return pl.pallas_call(
        functools.partial(_final_kernel, width=width),
        grid=(m // tm,),
        in_specs=[
            pl.BlockSpec((tm, d), lambda i: (i, 0)),
            pl.BlockSpec((tm, width), lambda i: (i, 0)),
            pl.BlockSpec((tm, width), lambda i: (i, 0)),
            pl.BlockSpec((tm, width), lambda i: (i, gsb_block)),
            pl.BlockSpec((tm, pd), lambda i: (i, 0)),
            const((2 * width, d)),
            const((1, d)),
            const((pd, d)),
            const((1, d)),
            const((d, d)),
        ],
        out_specs=pl.BlockSpec((tm, d), lambda i: (i, 0)),
        out_shape=jax.ShapeDtypeStruct((m, d), F32),
        compiler_params=pltpu.CompilerParams(
            dimension_semantics=("parallel",),
            vmem_limit_bytes=VMEM_LIMIT),
        name="out_proj_ple",
    )(x2, ya, yb, u2, p2, w_out, g_post.reshape(1, d), w_ple, g_ple.reshape(1, d), w_plg)


def kernel(x, p, g_pre, w_in, mu_shift, w0, w_up, a0, a_up, k_k, k_a, r_k, lnx_w, lnx_b,
           w_out, g_post, w_ple, w_plg, g_ple):
    bsz, seq, d = x.shape
    depth = w_in.shape[0]
    width = w0.shape[1]
    sw = mu_shift.shape[1]
    heads = width // HEAD_DIM
    lora = w_up.shape[1]
    m = bsz * seq
    h2 = x.reshape(m, d)
    for i in range(depth):
        w_in_b = w_in[i].astype(BF16)
        w_shift = w_in_b[:, :sw]
        w_rest = w_in_b[:, sw:]
        q_scale = jnp.ones((5 * width,), BF16).at[width:2 * width].set(HEAD_DIM ** -0.5)
        w_rest = w_rest * q_scale
        u1 = _norm_matmul(h2, g_pre[i], w_shift, tm=1024, tn=640)
        u2 = _norm_matmul(h2, g_pre[i], w_rest, tm=1024, tn=1024)

        zpad = jnp.zeros((LANES - lora, width), F32)
        wup_p = jnp.concatenate([w_up[i], zpad], axis=0).astype(BF16)
        aup_p = jnp.concatenate([zpad, a_up[i]], axis=0).astype(BF16)
        ya = _rwkv(u1.reshape(bsz, seq, sw), u2.reshape(bsz, seq, 5 * width), mu_shift[i],
                   w0[i], wup_p, a0[i], aup_p, k_k[i], k_a[i], r_k[i], lnx_w[i], lnx_b[i],
                   width)

        def heads_of(col):
            t = u2[:, col * width:(col + 1) * width].astype(BF16)
            return t.reshape(bsz, seq, heads, HEAD_DIM)
        qt = heads_of(1).transpose(0, 2, 3, 1)
        kh = heads_of(2).transpose(0, 2, 1, 3)
        vt = heads_of(3).transpose(0, 2, 3, 1)
        ot = _stickbreak(qt, kh, vt)
        yb = ot.transpose(0, 3, 1, 2).reshape(m, width)

        h2 = _final(h2, ya.reshape(m, width), yb, u2, p[i].reshape(m, -1),
                    w_out[i].astype(BF16), g_post[i], w_ple[i].astype(BF16), g_ple[i],
                    w_plg[i].astype(BF16), width, gsb_block=4, tm=512)
    return h2.reshape(bsz, seq, d)
```

```python
import functools

import jax
import jax.numpy as jnp
from jax import lax
from jax.experimental import pallas as pl
from jax.experimental.pallas import tpu as pltpu

F32 = jnp.float32
BF16 = jnp.bfloat16

HEAD_DIM = 64
LANES = 128
RMS_EPS = 1e-6
LNX_EPS = 64e-5
L2_EPS = 1e-12
CHUNK = 64
SB_TK = 128
SB_KG = 4
SB_TQ = SB_KG * SB_TK
LOG2E = 1.4426950408889634
VMEM_LIMIT = 56 * 1024 * 1024


def _dot(a, b):
    return jnp.dot(a, b, preferred_element_type=F32)


def _bdot(a, b):
    return lax.dot_general(a, b, (((2,), (1,)), ((0,), (0,))), preferred_element_type=F32)


def _bdot_nt(a, b):
    return lax.dot_general(a, b, (((2,), (2,)), ((0,), (0,))), preferred_element_type=F32)


def _bdot_tn(a, b):
    return lax.dot_general(a, b, (((1,), (1,)), ((0,), (0,))), preferred_element_type=F32)


def _split2(x):
    hi = x.astype(BF16)
    lo = (x - hi.astype(F32)).astype(BF16)
    return hi, lo


def _sigmoid(x):
    return 1.0 / (1.0 + jnp.exp(-x))


def _norm_matmul_kernel(x_ref, g_ref, w_ref, cs_ref, o_ref, xn_ref):
    @pl.when(pl.program_id(1) == 0)
    def _():
        x = x_ref[...]
        ms = jnp.mean(x * x, axis=-1, keepdims=True)
        xn_ref[...] = (x * lax.rsqrt(ms + RMS_EPS) * g_ref[...]).astype(BF16)

    o_ref[...] = _dot(xn_ref[...], w_ref[...]) * cs_ref[...]


def _norm_matmul(x2, gain, w, col_scale, tm, tn):
    m, k = x2.shape
    n = w.shape[1]
    return pl.pallas_call(
        _norm_matmul_kernel,
        grid=(m // tm, n // tn),
        in_specs=[
            pl.BlockSpec((tm, k), lambda i, j: (i, 0)),
            pl.BlockSpec((1, k), lambda i, j: (0, 0)),
            pl.BlockSpec((k, tn), lambda i, j: (0, j)),
            pl.BlockSpec((1, tn), lambda i, j: (0, j)),
        ],
        out_specs=pl.BlockSpec((tm, tn), lambda i, j: (i, j)),
        out_shape=jax.ShapeDtypeStruct((m, n), F32),
        scratch_shapes=[pltpu.VMEM((tm, k), BF16)],
        compiler_params=pltpu.CompilerParams(
            dimension_semantics=("parallel", "arbitrary"),
            vmem_limit_bytes=VMEM_LIMIT),
        name="norm_matmul",
    )(x2, gain.reshape(1, k), w, col_scale.reshape(1, n))


def _rwkv_kernel(u_ref, g_ref, mu_ref, w0_ref, wup_ref, a0_ref, aup_ref, kk_ref, ka_ref,
                 rk_ref, lw_ref, lb_ref, o_ref, prev_ref, st_ref, *, width):
    L = CHUNK
    nb = width // LANES

    @pl.when(pl.program_id(1) == 0)
    def _():
        prev_ref[...] = jnp.zeros_like(prev_ref)
        st_ref[...] = jnp.zeros_like(st_ref)

    u = u_ref[0]
    row = lax.broadcasted_iota(jnp.int32, u.shape, 0)
    u_prev = jnp.where(row == 0, prev_ref[0:1, :], pltpu.roll(u, 1, 0))
    prev_ref[0:1, :] = u[L - 1:L, :]
    us = u + (u_prev - u) * mu_ref[...]

    lane = lax.broadcasted_iota(jnp.int32, (L, LANES), 1)
    m0 = lane < HEAD_DIM
    xl = us[:, 3 * width:3 * width + LANES]
    xl = jnp.where(m0, jnp.tanh(xl), xl).astype(BF16)
    lora_w = _dot(xl, wup_ref[...])
    lora_a = _dot(xl, aup_ref[...])

    r2 = lax.broadcasted_iota(jnp.int32, (2 * L, 2 * L), 0)
    c2 = lax.broadcasted_iota(jnp.int32, (2 * L, 2 * L), 1)
    same = (r2 >= L) == (c2 >= L)
    strict_m = same & (r2 > c2)
    incl_m = same & (r2 >= c2)
    eye = r2 == c2
    eye_f = jnp.where(eye, 1.0, 0.0).astype(F32)
    bo = jnp.where(same, 1.0, 0.0).astype(BF16)
    block_ones2 = jnp.concatenate([bo, bo], axis=0)
    rl = lax.broadcasted_iota(jnp.int32, (L, 2 * L), 0)
    cl = lax.broadcasted_iota(jnp.int32, (L, 2 * L), 1)
    lower2 = jnp.where(rl >= (cl & (L - 1)), 1.0, 0.0).astype(BF16)

    def slabs(x):
        return jnp.stack([x[:, s * LANES:(s + 1) * LANES] for s in range(nb)], axis=0)

    def head_sum(x):
        hi, lo = _split2(x.reshape(nb * L, LANES))
        return _dot(jnp.concatenate([hi, lo], axis=1), block_ones2).reshape(nb, L, LANES)

    def stack(x):
        return jnp.concatenate([jnp.where(m0, x, 0.0), jnp.where(m0, 0.0, x)], axis=1)

    def bmm(a, b):
        return _bdot(a.astype(BF16), b.astype(BF16))

    r_f = us[:, 0:width]
    k_f = us[:, width:2 * width]
    v_f = us[:, 2 * width:3 * width]
    wpre = w0_ref[...] + lora_w
    softplus = jnp.maximum(-wpre, 0.0) + jnp.log(1.0 + jnp.exp(-jnp.abs(wpre)))
    logw = -jnp.exp(-softplus - 0.5)
    a_f = _sigmoid(a0_ref[...] + lora_a)
    k2_f = k_f * (1.0 + (a_f - 1.0) * ka_ref[...])
    lw_hi, lw_lo = _split2(logw)
    cum = _dot(lower2, jnp.concatenate([lw_hi, lw_lo], axis=0))
    p_in = jnp.exp(cum)
    q_in = jnp.exp(-cum)

    r = slabs(r_f)
    v = slabs(v_f)
    k2 = slabs(k2_f)
    a = slabs(a_f)
    kkr = slabs(k_f * kk_ref[...])
    kk = kkr * lax.rsqrt(head_sum(kkr * kkr) + L2_EPS)
    q3 = slabs(q_in)
    at = kk * slabs(jnp.exp(cum - logw))
    kt = k2 * q3
    bt = a * kk * q3
    rt = slabs(r_f * p_in)
    p_last = slabs(p_in[L - 1:L, :])

    a_s, k_s, b_s, r_s, v_s = stack(at), stack(kt), stack(bt), stack(rt), stack(v)
    kh_s, bh_s = stack(kt * p_last), stack(bt * p_last)

    prod = _bdot_nt(jnp.concatenate([a_s, r_s], axis=1).astype(BF16),
                    jnp.concatenate([k_s, b_s], axis=1).astype(BF16))
    akk = jnp.where(strict_m, prod[:, 0:2 * L, 0:2 * L], 0.0)
    aab = jnp.where(strict_m, prod[:, 0:2 * L, 2 * L:4 * L], 0.0)
    ark = jnp.where(incl_m, prod[:, 2 * L:4 * L, 0:2 * L], 0.0)
    arb = jnp.where(incl_m, prod[:, 2 * L:4 * L, 2 * L:4 * L], 0.0)

    tinv = eye_f - aab
    pw = bmm(aab, aab)
    n_sq = 1
    while True:
        tinv = tinv + bmm(tinv, pw)
        n_sq *= 2
        if 2 * n_sq >= L:
            break
        pw = bmm(pw, pw)

    akv = bmm(jnp.concatenate([akk, ark], axis=1), v_s)
    akk_v, ark_v = akv[:, 0:2 * L], akv[:, 2 * L:4 * L]
    au = bmm(tinv, jnp.concatenate([a_s, akk_v], axis=2))
    corr = bmm(arb, au)
    q_s = r_s - corr[:, :, 0:LANES]
    y0 = ark_v - corr[:, :, LANES:2 * LANES]
    gh_rhs = jnp.concatenate(
        [-au, jnp.concatenate([jnp.zeros_like(v_s), v_s], axis=2)], axis=1)
    gh = _bdot_tn(jnp.concatenate([bh_s, kh_s], axis=1).astype(BF16), gh_rhs.astype(BF16))
    g_mat = jnp.where(eye, p_last, 0.0) + gh[:, :, 0:LANES]
    h_mat = gh[:, :, LANES:2 * LANES]

    state = st_ref[...]
    upd = bmm(jnp.concatenate([q_s, g_mat], axis=1), state)
    st_ref[...] = upd[:, 2 * L:4 * L] + h_mat
    y_s = upd[:, 0:2 * L] + y0
    y = y_s[:, 0:L] + y_s[:, L:2 * L]

    mean = head_sum(y) * (1.0 / HEAD_DIM)
    yc = y - mean
    var = head_sum(yc * yc) * (1.0 / HEAD_DIM)
    yn = yc * lax.rsqrt(var + LNX_EPS) * slabs(lw_ref[...]) + slabs(lb_ref[...])
    bonus = head_sum(r * k2 * slabs(rk_ref[...])) * v
    g = slabs(g_ref[0])
    out = ((yn + bonus) * (g * _sigmoid(g))).astype(o_ref.dtype)
    for s in range(nb):
        o_ref[0, :, s * LANES:(s + 1) * LANES] = out[s]


def _rwkv(u1, u2, mu, w0, wup_p, a0, aup_p, k_k, k_a, r_k, lnx_w, lnx_b, width):
    b, s, sw = u1.shape
    n_slab = width // LANES
    row = lambda a: a.reshape(1, -1)
    vec = pl.BlockSpec((1, width), lambda i, j: (0, 0))
    return pl.pallas_call(
        functools.partial(_rwkv_kernel, width=width),
        grid=(b, s // CHUNK),
        in_specs=[
            pl.BlockSpec((1, CHUNK, sw), lambda i, j: (i, j, 0)),
            pl.BlockSpec((1, CHUNK, width), lambda i, j: (i, j, 0)),
            pl.BlockSpec((1, sw), lambda i, j: (0, 0)),
            vec,
            pl.BlockSpec((LANES, width), lambda i, j: (0, 0)),
            vec,
            pl.BlockSpec((LANES, width), lambda i, j: (0, 0)),
            vec, vec, vec, vec, vec,
        ],
        out_specs=pl.BlockSpec((1, CHUNK, width), lambda i, j: (i, j, 0)),
        out_shape=jax.ShapeDtypeStruct((b, s, width), BF16),
        scratch_shapes=[pltpu.VMEM((8, sw), F32),
                        pltpu.VMEM((n_slab, LANES, LANES), F32)],
        compiler_params=pltpu.CompilerParams(
            dimension_semantics=("parallel", "arbitrary"),
            vmem_limit_bytes=VMEM_LIMIT),
        name="rwkv7_chunked",
    )(u1, u2, row(mu), row(w0), wup_p, row(a0), aup_p, row(k_k), row(k_a), row(r_k),
      row(lnx_w), row(lnx_b))


def _stickbreak_kernel(qt_ref, k_ref, vt_ref, o_ref):
    i = pl.program_id(2)
    q = qt_ref[0, 0]
    rows = lax.broadcasted_iota(jnp.int32, (SB_TK, SB_TQ), 0)
    cols = lax.broadcasted_iota(jnp.int32, (SB_TK, SB_TQ), 1)
    ur = lax.broadcasted_iota(jnp.int32, (SB_TK, 2 * SB_TK), 0)
    uc = lax.broadcasted_iota(jnp.int32, (SB_TK, 2 * SB_TK), 1)
    upper2 = jnp.where((uc & (SB_TK - 1)) > ur, 1.0, 0.0).astype(BF16)

    def group(kbase, carry, acc, masked):
        blocks = range(SB_KG)
        k0s = [pl.multiple_of(kbase + m * SB_TK, SB_TK) for m in blocks]
        zs = [_dot(k_ref[0, 0, pl.ds(k0s[m], SB_TK), :], q) for m in blocks]
        log_betas, log_1mbs, splits, stricts = [], [], [], []
        for m in blocks:
            z = zs[m]
            neg_abs = lax.bitcast_convert_type(
                lax.bitcast_convert_type(z, jnp.uint32) | jnp.uint32(0x80000000), F32)
            soft = jnp.log2(1.0 + jnp.exp2(neg_abs))
            log_beta = jnp.minimum(z, 0.0) - soft
            log_1mb = log_beta - z
            if masked:
                strict = (k0s[m] + rows) < (i * SB_TQ + cols)
                log_1mb = jnp.where(strict, log_1mb, 0.0)
                stricts.append(strict)
            hi, lo = _split2(log_1mb)
            log_betas.append(log_beta)
            log_1mbs.append(log_1mb)
            splits.append(jnp.concatenate([hi, lo], axis=0))
        suffixes = [_dot(upper2, splits[m]) for m in blocks]
        prs, tots = [], []
        for m in blocks:
            tots.append(suffixes[m][0:1, :] + log_1mbs[m][0:1, :])
            pr = jnp.exp2(log_betas[m] + suffixes[m])
            if masked:
                pr = jnp.where(stricts[m], pr, 0.0)
            prs.append(pr.astype(BF16))
        pvs = [_dot(vt_ref[0, 0, :, pl.ds(k0s[m], SB_TK)], prs[m]) for m in blocks]
        for m in reversed(blocks):
            acc = acc + jnp.exp2(carry) * pvs[m]
            carry = carry + tots[m]
        return carry, acc

    zero = (jnp.zeros((1, SB_TQ), F32), jnp.zeros((HEAD_DIM, SB_TQ), F32))
    carry, acc = group(i * SB_TQ, zero[0], zero[1], True)

    def body(jj, carry_acc):
        return group((i - 1 - jj) * SB_TQ, carry_acc[0], carry_acc[1], False)

    _, acc = lax.fori_loop(0, i, body, (carry, acc))
    o_ref[0, 0] = acc


def _stickbreak(qt, kh, vt):
    b, h, d, s = qt.shape
    return pl.pallas_call(
        _stickbreak_kernel,
        grid=(b, h, s // SB_TQ),
        in_specs=[
            pl.BlockSpec((1, 1, d, SB_TQ), lambda bi, hi, qi: (bi, hi, 0, qi)),
            pl.BlockSpec((1, 1, s, d), lambda bi, hi, qi: (bi, hi, 0, 0)),
            pl.BlockSpec((1, 1, d, s), lambda bi, hi, qi: (bi, hi, 0, 0)),
        ],
        out_specs=pl.BlockSpec((1, 1, d, SB_TQ), lambda bi, hi, qi: (bi, hi, 0, qi)),
        out_shape=jax.ShapeDtypeStruct((b, h, d, s), F32),
        compiler_params=pltpu.CompilerParams(
            dimension_semantics=("parallel", "parallel", "arbitrary"),
            vmem_limit_bytes=VMEM_LIMIT),
        name="stickbreak_attn",
    )(qt, kh, vt)


def _final_kernel(x_ref, ya_ref, yb_ref, gsb_ref, p_ref, wout_ref, gpost_ref, wple_ref,
                  gple_ref, wplg_ref, o_ref, *, width):
    def rms(t, gain):
        return t * lax.rsqrt(jnp.mean(t * t, axis=-1, keepdims=True) + RMS_EPS) * gain

    gs = gsb_ref[...]
    yb = (yb_ref[...] * (gs * _sigmoid(gs))).astype(BF16)
    mix = _dot(ya_ref[...], wout_ref[0:width, :]) + _dot(yb, wout_ref[width:, :])
    h = x_ref[...] + rms(mix, gpost_ref[...])
    ple = rms(_dot(p_ref[...].astype(BF16), wple_ref[...]), gple_ref[...])
    gate = _sigmoid(_dot(h.astype(BF16), wplg_ref[...]))
    o_ref[...] = h + ple * gate


def _final(x2, ya, yb, u2, p2, w_out, g_post, w_ple, g_ple, w_plg, width, gsb_block, tm):
    m, d = x2.shape
    pd = p2.shape[1]
    const = lambda shape: pl.BlockSpec(shape, lambda i: (0, 0), pipeline_mode=pl.Buffered(1))
    return pl.pallas_call(
        functools.partial(_final_kernel, width=width),
        grid=(m // tm,),
        in_specs=[
            pl.BlockSpec((tm, d), lambda i: (i, 0)),
            pl.BlockSpec((tm, width), lambda i: (i, 0)),
            pl.BlockSpec((tm, width), lambda i: (i, 0)),
            pl.BlockSpec((tm, width), lambda i: (i, gsb_block)),
            pl.BlockSpec((tm, pd), lambda i: (i, 0)),
            const((2 * width, d)),
            const((1, d)),
            const((pd, d)),
            const((1, d)),
            const((d, d)),
        ],
        out_specs=pl.BlockSpec((tm, d), lambda i: (i, 0)),
        out_shape=jax.ShapeDtypeStruct((m, d), F32),
        compiler_params=pltpu.CompilerParams(
            dimension_semantics=("parallel",),
            vmem_limit_bytes=VMEM_LIMIT),
        name="out_proj_ple",
    )(x2, ya, yb, u2, p2, w_out, g_post.reshape(1, d), w_ple, g_ple.reshape(1, d), w_plg)


def kernel(x, p, g_pre, w_in, mu_shift, w0, w_up, a0, a_up, k_k, k_a, r_k, lnx_w, lnx_b,
           w_out, g_post, w_ple, w_plg, g_ple):
    bsz, seq, d = x.shape
    depth = w_in.shape[0]
    width = w0.shape[1]
    sw = mu_shift.shape[1]
    heads = width // HEAD_DIM
    lora = w_up.shape[1]
    m = bsz * seq
    h2 = x.reshape(m, d)
    for i in range(depth):
        w_in_b = w_in[i].astype(BF16)
        q_scale = jnp.ones((5 * width,), F32).at[width:2 * width].set(LOG2E * HEAD_DIM ** -0.5)
        u1 = _norm_matmul(h2, g_pre[i], w_in_b[:, :sw], jnp.ones((sw,), F32), tm=1024, tn=640)
        u2 = _norm_matmul(h2, g_pre[i], w_in_b[:, sw:], q_scale, tm=1024, tn=1024)

        zpad = jnp.zeros((LANES - lora, width), F32)
        wup_p = jnp.concatenate([w_up[i], zpad], axis=0).astype(BF16)
        aup_p = jnp.concatenate([zpad, a_up[i]], axis=0).astype(BF16)
        ya = _rwkv(u1.reshape(bsz, seq, sw), u2.reshape(bsz, seq, 5 * width), mu_shift[i],
                   w0[i], wup_p, a0[i], aup_p, k_k[i], k_a[i], r_k[i], lnx_w[i], lnx_b[i],
                   width)

        def heads_of(col):
            t = u2[:, col * width:(col + 1) * width].astype(BF16)
            return t.reshape(bsz, seq, heads, HEAD_DIM)
        qt = heads_of(1).transpose(0, 2, 3, 1)
        kh = heads_of(2).transpose(0, 2, 1, 3)
        vt = heads_of(3).transpose(0, 2, 3, 1)
        ot = _stickbreak(qt, kh, vt)
        yb = ot.transpose(0, 3, 1, 2).reshape(m, width)

        h2 = _final(h2, ya.reshape(m, width), yb, u2, p[i].reshape(m, -1),
                    w_out[i].astype(BF16), g_post[i], w_ple[i].astype(BF16), g_ple[i],
                    w_plg[i].astype(BF16), width, gsb_block=4, tm=512)
    return h2.reshape(bsz, seq, d)
```

```python
import functools

import jax
import jax.numpy as jnp
from jax import lax
from jax.experimental import pallas as pl
from jax.experimental.pallas import tpu as pltpu

F32 = jnp.float32
BF16 = jnp.bfloat16

HEAD_DIM = 64
LANES = 128
RMS_EPS = 1e-6
LNX_EPS = 64e-5
L2_EPS = 1e-12
CHUNK = 64
SB_TK = 128
SB_KG = 4
SB_TQ = SB_KG * SB_TK
LOG2E = 1.4426950408889634
VMEM_LIMIT = 56 * 1024 * 1024


def _dot(a, b):
    return jnp.dot(a, b, preferred_element_type=F32)


def _bdot(a, b):
    return lax.dot_general(a, b, (((2,), (1,)), ((0,), (0,))), preferred_element_type=F32)


def _bdot_nt(a, b):
    return lax.dot_general(a, b, (((2,), (2,)), ((0,), (0,))), preferred_element_type=F32)


def _bdot_tn(a, b):
    return lax.dot_general(a, b, (((1,), (1,)), ((0,), (0,))), preferred_element_type=F32)


def _split2(x):
    hi = x.astype(BF16)
    lo = (x - hi.astype(F32)).astype(BF16)
    return hi, lo


def _sigmoid(x):
    return 1.0 / (1.0 + jnp.exp(-x))


def _rms_bf16(x, gain):
    ms = jnp.mean(x * x, axis=-1, keepdims=True)
    return (x * lax.rsqrt(ms + RMS_EPS) * gain).astype(BF16)


def _norm_matmul_kernel(x_ref, g_ref, w_ref, o_ref):
    o_ref[...] = _dot(_rms_bf16(x_ref[...], g_ref[...]), w_ref[...])


def _norm_matmul(x2, gain, w, tm):
    m, k = x2.shape
    n = w.shape[1]
    return pl.pallas_call(
        _norm_matmul_kernel,
        grid=(m // tm,),
        in_specs=[
            pl.BlockSpec((tm, k), lambda i: (i, 0)),
            pl.BlockSpec((1, k), lambda i: (0, 0)),
            pl.BlockSpec((k, n), lambda i: (0, 0), pipeline_mode=pl.Buffered(1)),
        ],
        out_specs=pl.BlockSpec((tm, n), lambda i: (i, 0)),
        out_shape=jax.ShapeDtypeStruct((m, n), F32),
        compiler_params=pltpu.CompilerParams(
            dimension_semantics=("parallel",),
            vmem_limit_bytes=VMEM_LIMIT),
        name="norm_matmul_shift",
    )(x2, gain.reshape(1, k), w)


def _norm_matmul_split_kernel(x_ref, g_ref, w_ref, gate_ref, qkv_ref, xn_ref, *, q_scale):
    j = pl.program_id(1)
    last = pl.num_programs(1) - 1

    @pl.when(j == 0)
    def _():
        xn_ref[...] = _rms_bf16(x_ref[...], g_ref[...])

    acc = _dot(xn_ref[...], w_ref[...])
    is_gate = jnp.logical_or(j == 0, j == last)

    @pl.when(is_gate)
    def _():
        gate_ref[...] = acc

    @pl.when(j == 1)
    def _():
        qkv_ref[...] = (acc * q_scale).astype(BF16)

    @pl.when(jnp.logical_and(j > 1, j < last))
    def _():
        qkv_ref[...] = acc.astype(BF16)


def _norm_matmul_split(x2, gain, w, q_scale, tm, tn):
    m, k = x2.shape
    n_tiles = w.shape[1] // tn
    assert n_tiles == 5
    return pl.pallas_call(
        functools.partial(_norm_matmul_split_kernel, q_scale=q_scale),
        grid=(m // tm, n_tiles),
        in_specs=[
            pl.BlockSpec((tm, k), lambda i, j: (i, 0)),
            pl.BlockSpec((1, k), lambda i, j: (0, 0)),
            pl.BlockSpec((k, tn), lambda i, j: (0, j)),
        ],
        out_specs=[
            pl.BlockSpec((tm, tn), lambda i, j: (i, j // (n_tiles - 1))),
            pl.BlockSpec((tm, tn), lambda i, j: (i, jnp.clip(j - 1, 0, n_tiles - 3))),
        ],
        out_shape=[jax.ShapeDtypeStruct((m, 2 * tn), F32),
                   jax.ShapeDtypeStruct((m, 3 * tn), BF16)],
        scratch_shapes=[pltpu.VMEM((tm, k), BF16)],
        compiler_params=pltpu.CompilerParams(
            dimension_semantics=("parallel", "arbitrary"),
            vmem_limit_bytes=VMEM_LIMIT),
        name="norm_matmul_split",
    )(x2, gain.reshape(1, k), w)


def _rwkv_kernel(u_ref, g_ref, mu_ref, w0_ref, wup_ref, a0_ref, aup_ref, kk_ref, ka_ref,
                 rk_ref, lw_ref, lb_ref, o_ref, prev_ref, st_ref, *, width):
    L = CHUNK
    nb = width // LANES

    @pl.when(pl.program_id(1) == 0)
    def _():
        prev_ref[...] = jnp.zeros_like(prev_ref)
        st_ref[...] = jnp.zeros_like(st_ref)

    u = u_ref[0]
    row = lax.broadcasted_iota(jnp.int32, u.shape, 0)
    u_prev = jnp.where(row == 0, prev_ref[0:1, :], pltpu.roll(u, 1, 0))
    prev_ref[0:1, :] = u[L - 1:L, :]
    us = u + (u_prev - u) * mu_ref[...]

    lane = lax.broadcasted_iota(jnp.int32, (L, LANES), 1)
    m0 = lane < HEAD_DIM
    xl = us[:, 3 * width:3 * width + LANES]
    xl = jnp.where(m0, jnp.tanh(xl), xl).astype(BF16)
    lora_w = _dot(xl, wup_ref[...])
    lora_a = _dot(xl, aup_ref[...])

    r2 = lax.broadcasted_iota(jnp.int32, (2 * L, 2 * L), 0)
    c2 = lax.broadcasted_iota(jnp.int32, (2 * L, 2 * L), 1)
    same = (r2 >= L) == (c2 >= L)
    strict_m = same & (r2 > c2)
    incl_m = same & (r2 >= c2)
    eye = r2 == c2
    eye_f = jnp.where(eye, 1.0, 0.0).astype(F32)
    bo = jnp.where(same, 1.0, 0.0).astype(BF16)
    block_ones2 = jnp.concatenate([bo, bo], axis=0)
    rl = lax.broadcasted_iota(jnp.int32, (L, 2 * L), 0)
    cl = lax.broadcasted_iota(jnp.int32, (L, 2 * L), 1)
    lower2 = jnp.where(rl >= (cl & (L - 1)), 1.0, 0.0).astype(BF16)

    def slabs(x):
        return jnp.stack([x[:, s * LANES:(s + 1) * LANES] for s in range(nb)], axis=0)

    def head_sum(x):
        hi, lo = _split2(x.reshape(nb * L, LANES))
        return _dot(jnp.concatenate([hi, lo], axis=1), block_ones2).reshape(nb, L, LANES)

    def stack(x):
        return jnp.concatenate([jnp.where(m0, x, 0.0), jnp.where(m0, 0.0, x)], axis=1)

    def bmm(a, b):
        return _bdot(a.astype(BF16), b.astype(BF16))

    r_f = us[:, 0:width]
    k_f = us[:, width:2 * width]
    v_f = us[:, 2 * width:3 * width]
    wpre = w0_ref[...] + lora_w
    softplus = jnp.maximum(-wpre, 0.0) + jnp.log(1.0 + jnp.exp(-jnp.abs(wpre)))
    logw = -jnp.exp(-softplus - 0.5)
    a_f = _sigmoid(a0_ref[...] + lora_a)
    k2_f = k_f * (1.0 + (a_f - 1.0) * ka_ref[...])
    lw_hi, lw_lo = _split2(logw)
    cum = _dot(lower2, jnp.concatenate([lw_hi, lw_lo], axis=0))
    p_in = jnp.exp(cum)
    q_in = jnp.exp(-cum)

    r = slabs(r_f)
    v = slabs(v_f)
    k2 = slabs(k2_f)
    a = slabs(a_f)
    kkr = slabs(k_f * kk_ref[...])
    kk = kkr * lax.rsqrt(head_sum(kkr * kkr) + L2_EPS)
    q3 = slabs(q_in)
    at = kk * slabs(jnp.exp(cum - logw))
    kt = k2 * q3
    bt = a * kk * q3
    rt = slabs(r_f * p_in)
    p_last = slabs(p_in[L - 1:L, :])

    a_s, k_s, b_s, r_s, v_s = stack(at), stack(kt), stack(bt), stack(rt), stack(v)
    kh_s, bh_s = stack(kt * p_last), stack(bt * p_last)

    prod = _bdot_nt(jnp.concatenate([a_s, r_s], axis=1).astype(BF16),
                    jnp.concatenate([k_s, b_s], axis=1).astype(BF16))
    akk = jnp.where(strict_m, prod[:, 0:2 * L, 0:2 * L], 0.0)
    aab = jnp.where(strict_m, prod[:, 0:2 * L, 2 * L:4 * L], 0.0)
    ark = jnp.where(incl_m, prod[:, 2 * L:4 * L, 0:2 * L], 0.0)
    arb = jnp.where(incl_m, prod[:, 2 * L:4 * L, 2 * L:4 * L], 0.0)

    tinv = eye_f - aab
    pw = bmm(aab, aab)
    n_sq = 1
    while True:
        tinv = tinv + bmm(tinv, pw)
        n_sq *= 2
        if 2 * n_sq >= L:
            break
        pw = bmm(pw, pw)

    akv = bmm(jnp.concatenate([akk, ark], axis=1), v_s)
    akk_v, ark_v = akv[:, 0:2 * L], akv[:, 2 * L:4 * L]
    au = bmm(tinv, jnp.concatenate([a_s, akk_v], axis=2))
    corr = bmm(arb, au)
    q_s = r_s - corr[:, :, 0:LANES]
    y0 = ark_v - corr[:, :, LANES:2 * LANES]
    gh_rhs = jnp.concatenate(
        [-au, jnp.concatenate([jnp.zeros_like(v_s), v_s], axis=2)], axis=1)
    gh = _bdot_tn(jnp.concatenate([bh_s, kh_s], axis=1).astype(BF16), gh_rhs.astype(BF16))
    g_mat = jnp.where(eye, p_last, 0.0) + gh[:, :, 0:LANES]
    h_mat = gh[:, :, LANES:2 * LANES]

    state = st_ref[...]
    upd = bmm(jnp.concatenate([q_s, g_mat], axis=1), state)
    st_ref[...] = upd[:, 2 * L:4 * L] + h_mat
    y_s = upd[:, 0:2 * L] + y0
    y = y_s[:, 0:L] + y_s[:, L:2 * L]

    mean = head_sum(y) * (1.0 / HEAD_DIM)
    yc = y - mean
    var = head_sum(yc * yc) * (1.0 / HEAD_DIM)
    yn = yc * lax.rsqrt(var + LNX_EPS) * slabs(lw_ref[...]) + slabs(lb_ref[...])
    bonus = head_sum(r * k2 * slabs(rk_ref[...])) * v
    g = slabs(g_ref[0])
    out = ((yn + bonus) * (g * _sigmoid(g))).astype(o_ref.dtype)
    for s in range(nb):
        o_ref[0, :, s * LANES:(s + 1) * LANES] = out[s]


def _rwkv(u1, u2, mu, w0, wup_p, a0, aup_p, k_k, k_a, r_k, lnx_w, lnx_b, width):
    b, s, sw = u1.shape
    n_slab = width // LANES
    row = lambda a: a.reshape(1, -1)
    vec = pl.BlockSpec((1, width), lambda i, j: (0, 0))
    return pl.pallas_call(
        functools.partial(_rwkv_kernel, width=width),
        grid=(b, s // CHUNK),
        in_specs=[
            pl.BlockSpec((1, CHUNK, sw), lambda i, j: (i, j, 0)),
            pl.BlockSpec((1, CHUNK, width), lambda i, j: (i, j, 0)),
            pl.BlockSpec((1, sw), lambda i, j: (0, 0)),
            vec,
            pl.BlockSpec((LANES, width), lambda i, j: (0, 0)),
            vec,
            pl.BlockSpec((LANES, width), lambda i, j: (0, 0)),
            vec, vec, vec, vec, vec,
        ],
        out_specs=pl.BlockSpec((1, CHUNK, width), lambda i, j: (i, j, 0)),
        out_shape=jax.ShapeDtypeStruct((b, s, width), BF16),
        scratch_shapes=[pltpu.VMEM((8, sw), F32),
                        pltpu.VMEM((n_slab, LANES, LANES), F32)],
        compiler_params=pltpu.CompilerParams(
            dimension_semantics=("parallel", "arbitrary"),
            vmem_limit_bytes=VMEM_LIMIT),
        name="rwkv7_chunked",
    )(u1, u2, row(mu), row(w0), wup_p, row(a0), aup_p, row(k_k), row(k_a), row(r_k),
      row(lnx_w), row(lnx_b))


def _stickbreak_kernel(q_ref, k_ref, v_ref, o_ref, vt_ref):
    i = pl.program_id(2)
    seq = k_ref.shape[1]
    heads = range(LANES // HEAD_DIM)
    blocks = range(SB_KG)

    @pl.when(i == 0)
    def _():
        def transpose_chunk(c, _):
            c0 = pl.multiple_of(c * SB_TQ, SB_TQ)
            vt_ref[:, pl.ds(c0, SB_TQ)] = v_ref[0, pl.ds(c0, SB_TQ), :].astype(F32).T.astype(BF16)
            return 0
        lax.fori_loop(0, seq // SB_TQ, transpose_chunk, 0)

    qt = q_ref[0].astype(F32).T
    hrow = lax.broadcasted_iota(jnp.int32, qt.shape, 0)
    qs = [jnp.where((hrow >= h * HEAD_DIM) & (hrow < (h + 1) * HEAD_DIM), qt, 0.0).astype(BF16)
          for h in heads]
    rows = lax.broadcasted_iota(jnp.int32, (SB_TK, SB_TQ), 0)
    cols = lax.broadcasted_iota(jnp.int32, (SB_TK, SB_TQ), 1)
    ur = lax.broadcasted_iota(jnp.int32, (SB_TK, SB_TK), 0)
    uc = lax.broadcasted_iota(jnp.int32, (SB_TK, SB_TK), 1)
    upper = jnp.where(uc > ur, 1.0, 0.0).astype(BF16)

    def group(kbase, state, masked):
        carries, accs = list(state[:2]), list(state[2:])
        k0s = [pl.multiple_of(kbase + m * SB_TK, SB_TK) for m in blocks]
        kbs = [k_ref[0, pl.ds(k0s[m], SB_TK), :] for m in blocks]
        zs = [[_dot(kbs[m], qs[h]) for m in blocks] for h in heads]
        stricts = [(k0s[m] + rows) < (i * SB_TQ + cols) for m in blocks] if masked else None
        log_betas = [[None] * SB_KG for _ in heads]
        log_1mbs = [[None] * SB_KG for _ in heads]
        for h in heads:
            for m in blocks:
                z = zs[h][m]
                soft = jnp.log2(1.0 + jnp.exp2(-jnp.abs(z)))
                log_beta = jnp.minimum(z, 0.0) - soft
                log_1mb = log_beta - z
                if masked:
                    log_1mb = jnp.where(stricts[m], log_1mb, 0.0)
                log_betas[h][m] = log_beta
                log_1mbs[h][m] = log_1mb
        suffixes = [[_dot(upper, log_1mbs[h][m].astype(BF16)) for m in blocks] for h in heads]
        prs = [[None] * SB_KG for _ in heads]
        tots = [[None] * SB_KG for _ in heads]
        for h in heads:
            for m in blocks:
                tots[h][m] = suffixes[h][m][0:1, :] + log_1mbs[h][m][0:1, :]
                pr = jnp.exp2(log_betas[h][m] + suffixes[h][m])
                if masked:
                    pr = jnp.where(stricts[m], pr, 0.0)
                prs[h][m] = pr.astype(BF16)
        pvs = [[_dot(vt_ref[h * HEAD_DIM:(h + 1) * HEAD_DIM, pl.ds(k0s[m], SB_TK)], prs[h][m])
                for m in blocks] for h in heads]
        for h in heads:
            for m in reversed(blocks):
                accs[h] = accs[h] + jnp.exp2(carries[h]) * pvs[h][m]
                carries[h] = carries[h] + tots[h][m]
        return tuple(carries) + tuple(accs)

    zero_c = jnp.zeros((1, SB_TQ), F32)
    zero_a = jnp.zeros((HEAD_DIM, SB_TQ), F32)
    state = group(i * SB_TQ, (zero_c, zero_c, zero_a, zero_a), True)
    state = lax.fori_loop(0, i, lambda jj, st: group((i - 1 - jj) * SB_TQ, st, False), state)
    o_ref[0] = jnp.concatenate(state[2:], axis=0).T


def _stickbreak(qkv, width):
    b, s, _ = qkv.shape
    n_pair = width // LANES
    return pl.pallas_call(
        _stickbreak_kernel,
        grid=(b, n_pair, s // SB_TQ),
        in_specs=[
            pl.BlockSpec((1, SB_TQ, LANES), lambda bi, pi, qi: (bi, qi, pi)),
            pl.BlockSpec((1, s, LANES), lambda bi, pi, qi: (bi, 0, n_pair + pi)),
            pl.BlockSpec((1, s, LANES), lambda bi, pi, qi: (bi, 0, 2 * n_pair + pi)),
        ],
        out_specs=pl.BlockSpec((1, SB_TQ, LANES), lambda bi, pi, qi: (bi, qi, pi)),
        out_shape=jax.ShapeDtypeStruct((b, s, width), F32),
        scratch_shapes=[pltpu.VMEM((LANES, s), BF16)],
        compiler_params=pltpu.CompilerParams(
            dimension_semantics=("arbitrary", "arbitrary", "arbitrary"),
            vmem_limit_bytes=VMEM_LIMIT),
        name="stickbreak_attn",
    )(qkv, qkv, qkv)


def _final_kernel(x_ref, ya_ref, yb_ref, gsb_ref, p_ref, wout_ref, gpost_ref, wple_ref,
                  gple_ref, wplg_ref, o_ref, *, width):
    def rms(t, gain):
        return t * lax.rsqrt(jnp.mean(t * t, axis=-1, keepdims=True) + RMS_EPS) * gain

    gs = gsb_ref[...]
    yb = (yb_ref[...] * (gs * _sigmoid(gs))).astype(BF16)
    mix = _dot(ya_ref[...], wout_ref[0:width, :]) + _dot(yb, wout_ref[width:, :])
    h = x_ref[...] + rms(mix, gpost_ref[...])
    ple = rms(_dot(p_ref[...].astype(BF16), wple_ref[...]), gple_ref[...])
    gate = _sigmoid(_dot(h.astype(BF16), wplg_ref[...]))
    o_ref[...] = h + ple * gate


def _final(x2, ya, yb, u2, p2, w_out, g_post, w_ple, g_ple, w_plg, width, gsb_block, tm):
    m, d = x2.shape
    pd = p2.shape[1]
    const = lambda shape: pl.BlockSpec(shape, lambda i: (0, 0), pipeline_mode=pl.Buffered(1))
    return pl.pallas_call(
        functools.partial(_final_kernel, width=width),
        grid=(m // tm,),
        in_specs=[
            pl.BlockSpec((tm, d), lambda i: (i, 0)),
            pl.BlockSpec((tm, width), lambda i: (i, 0)),
            pl.BlockSpec((tm, width), lambda i: (i, 0)),
            pl.BlockSpec((tm, width), lambda i: (i, gsb_block)),
            pl.BlockSpec((tm, pd), lambda i: (i, 0)),
            const((2 * width, d)),
            const((1, d)),
            const((pd, d)),
            const((1, d)),
            const((d, d)),
        ],
        out_specs=pl.BlockSpec((tm, d), lambda i: (i, 0)),
        out_shape=jax.ShapeDtypeStruct((m, d), F32),
        compiler_params=pltpu.CompilerParams(
            dimension_semantics=("parallel",),
            vmem_limit_bytes=VMEM_LIMIT),
        name="out_proj_ple",
    )(x2, ya, yb, u2, p2, w_out, g_post.reshape(1, d), w_ple, g_ple.reshape(1, d), w_plg)


def kernel(x, p, g_pre, w_in, mu_shift, w0, w_up, a0, a_up, k_k, k_a, r_k, lnx_w, lnx_b,
           w_out, g_post, w_ple, w_plg, g_ple):
    bsz, seq, d = x.shape
    depth = w_in.shape[0]
    width = w0.shape[1]
    sw = mu_shift.shape[1]
    lora = w_up.shape[1]
    m = bsz * seq
    h2 = x.reshape(m, d)
    for i in range(depth):
        w_in_b = w_in[i].astype(BF16)
        u1 = _norm_matmul(h2, g_pre[i], w_in_b[:, :sw], tm=512)
        gates, qkv = _norm_matmul_split(h2, g_pre[i], w_in_b[:, sw:],
                                        LOG2E * HEAD_DIM ** -0.5, tm=1024, tn=width)

        zpad = jnp.zeros((LANES - lora, width), F32)
        wup_p = jnp.concatenate([w_up[i], zpad], axis=0).astype(BF16)
        aup_p = jnp.concatenate([zpad, a_up[i]], axis=0).astype(BF16)
        ya = _rwkv(u1.reshape(bsz, seq, sw), gates.reshape(bsz, seq, 2 * width), mu_shift[i],
                   w0[i], wup_p, a0[i], aup_p, k_k[i], k_a[i], r_k[i], lnx_w[i], lnx_b[i],
                   width)
        yb = _stickbreak(qkv.reshape(bsz, seq, 3 * width), width)
        h2 = _final(h2, ya.reshape(m, width), yb.reshape(m, width), gates, p[i].reshape(m, -1),
                    w_out[i].astype(BF16), g_post[i], w_ple[i].astype(BF16), g_ple[i],
                    w_plg[i].astype(BF16), width, gsb_block=1, tm=512)
    return h2.reshape(bsz, seq, d)
```

```python
import functools

import jax
import jax.numpy as jnp
from jax import lax
from jax.experimental import pallas as pl
from jax.experimental.pallas import tpu as pltpu

F32 = jnp.float32
BF16 = jnp.bfloat16

HEAD_DIM = 64
LANES = 128
RMS_EPS = 1e-6
LNX_EPS = 64e-5
L2_EPS = 1e-12
CHUNK = 64
SB_TK = 128
SB_KG = 4
SB_TQ = SB_KG * SB_TK
LOG2E = 1.4426950408889634
SB_DEAD_LOG2 = -160.0
VMEM_LIMIT = 56 * 1024 * 1024


def _dot(a, b):
    return jnp.dot(a, b, preferred_element_type=F32)


def _bdot(a, b):
    return lax.dot_general(a, b, (((2,), (1,)), ((0,), (0,))), preferred_element_type=F32)


def _bdot_nt(a, b):
    return lax.dot_general(a, b, (((2,), (2,)), ((0,), (0,))), preferred_element_type=F32)


def _bdot_tn(a, b):
    return lax.dot_general(a, b, (((1,), (1,)), ((0,), (0,))), preferred_element_type=F32)


def _split2(x):
    hi = x.astype(BF16)
    lo = (x - hi.astype(F32)).astype(BF16)
    return hi, lo


def _sigmoid(x):
    return 1.0 / (1.0 + jnp.exp(-x))


def _rms_bf16(x, gain):
    ms = jnp.mean(x * x, axis=-1, keepdims=True)
    return (x * lax.rsqrt(ms + RMS_EPS) * gain).astype(BF16)


def _norm_matmul_kernel(x_ref, g_ref, w_ref, o_ref):
    o_ref[...] = _dot(_rms_bf16(x_ref[...], g_ref[...]), w_ref[...])


def _norm_matmul(x2, gain, w, tm):
    m, k = x2.shape
    n = w.shape[1]
    return pl.pallas_call(
        _norm_matmul_kernel,
        grid=(m // tm,),
        in_specs=[
            pl.BlockSpec((tm, k), lambda i: (i, 0)),
            pl.BlockSpec((1, k), lambda i: (0, 0)),
            pl.BlockSpec((k, n), lambda i: (0, 0), pipeline_mode=pl.Buffered(1)),
        ],
        out_specs=pl.BlockSpec((tm, n), lambda i: (i, 0)),
        out_shape=jax.ShapeDtypeStruct((m, n), F32),
        compiler_params=pltpu.CompilerParams(
            dimension_semantics=("parallel",),
            vmem_limit_bytes=VMEM_LIMIT),
        name="norm_matmul_shift",
    )(x2, gain.reshape(1, k), w)


def _norm_matmul_split_kernel(x_ref, g_ref, w_ref, gate_ref, qkv_ref, xn_ref, *, q_scale):
    j = pl.program_id(1)
    last = pl.num_programs(1) - 1

    @pl.when(j == 0)
    def _():
        xn_ref[...] = _rms_bf16(x_ref[...], g_ref[...])

    acc = _dot(xn_ref[...], w_ref[...])
    is_gate = jnp.logical_or(j == 0, j == last)

    @pl.when(is_gate)
    def _():
        gate_ref[...] = acc

    @pl.when(j == 1)
    def _():
        qkv_ref[...] = (acc * q_scale).astype(BF16)

    @pl.when(jnp.logical_and(j > 1, j < last))
    def _():
        qkv_ref[...] = acc.astype(BF16)


def _norm_matmul_split(x2, gain, w, q_scale, tm, tn):
    m, k = x2.shape
    n_tiles = w.shape[1] // tn
    assert n_tiles == 5
    return pl.pallas_call(
        functools.partial(_norm_matmul_split_kernel, q_scale=q_scale),
        grid=(m // tm, n_tiles),
        in_specs=[
            pl.BlockSpec((tm, k), lambda i, j: (i, 0)),
            pl.BlockSpec((1, k), lambda i, j: (0, 0)),
            pl.BlockSpec((k, tn), lambda i, j: (0, j)),
        ],
        out_specs=[
            pl.BlockSpec((tm, tn), lambda i, j: (i, j // (n_tiles - 1))),
            pl.BlockSpec((tm, tn), lambda i, j: (i, jnp.clip(j - 1, 0, n_tiles - 3))),
        ],
        out_shape=[jax.ShapeDtypeStruct((m, 2 * tn), F32),
                   jax.ShapeDtypeStruct((m, 3 * tn), BF16)],
        scratch_shapes=[pltpu.VMEM((tm, k), BF16)],
        compiler_params=pltpu.CompilerParams(
            dimension_semantics=("parallel", "arbitrary"),
            vmem_limit_bytes=VMEM_LIMIT),
        name="norm_matmul_split",
    )(x2, gain.reshape(1, k), w)


def _rwkv_kernel(u_ref, g_ref, mu_ref, w0_ref, wup_ref, a0_ref, aup_ref, kk_ref, ka_ref,
                 rk_ref, lw_ref, lb_ref, o_ref, prev_ref, st_ref, *, width):
    L = CHUNK
    nb = width // LANES

    @pl.when(pl.program_id(1) == 0)
    def _():
        prev_ref[...] = jnp.zeros_like(prev_ref)
        st_ref[...] = jnp.zeros_like(st_ref)

    u = u_ref[0]
    row = lax.broadcasted_iota(jnp.int32, u.shape, 0)
    u_prev = jnp.where(row == 0, prev_ref[0:1, :], pltpu.roll(u, 1, 0))
    prev_ref[0:1, :] = u[L - 1:L, :]
    us = u + (u_prev - u) * mu_ref[...]

    lane = lax.broadcasted_iota(jnp.int32, (L, LANES), 1)
    m0 = lane < HEAD_DIM
    xl = us[:, 3 * width:3 * width + LANES]
    xl = jnp.where(m0, jnp.tanh(xl), xl).astype(BF16)
    lora_w = _dot(xl, wup_ref[...])
    lora_a = _dot(xl, aup_ref[...])

    r2 = lax.broadcasted_iota(jnp.int32, (2 * L, 2 * L), 0)
    c2 = lax.broadcasted_iota(jnp.int32, (2 * L, 2 * L), 1)
    same = (r2 >= L) == (c2 >= L)
    strict_m = same & (r2 > c2)
    incl_m = same & (r2 >= c2)
    eye = r2 == c2
    eye_f = jnp.where(eye, 1.0, 0.0).astype(F32)
    bo = jnp.where(same, 1.0, 0.0).astype(BF16)
    block_ones2 = jnp.concatenate([bo, bo], axis=0)
    rl = lax.broadcasted_iota(jnp.int32, (L, 2 * L), 0)
    cl = lax.broadcasted_iota(jnp.int32, (L, 2 * L), 1)
    lower2 = jnp.where(rl >= (cl & (L - 1)), 1.0, 0.0).astype(BF16)

    def slabs(x):
        return jnp.stack([x[:, s * LANES:(s + 1) * LANES] for s in range(nb)], axis=0)

    def head_sum(x):
        hi, lo = _split2(x.reshape(nb * L, LANES))
        return _dot(jnp.concatenate([hi, lo], axis=1), block_ones2).reshape(nb, L, LANES)

    def stack(x):
        return jnp.concatenate([jnp.where(m0, x, 0.0), jnp.where(m0, 0.0, x)], axis=1)

    def bmm(a, b):
        return _bdot(a.astype(BF16), b.astype(BF16))

    r_f = us[:, 0:width]
    k_f = us[:, width:2 * width]
    v_f = us[:, 2 * width:3 * width]
    wpre = w0_ref[...] + lora_w
    softplus = jnp.maximum(-wpre, 0.0) + jnp.log(1.0 + jnp.exp(-jnp.abs(wpre)))
    logw = -jnp.exp(-softplus - 0.5)
    a_f = _sigmoid(a0_ref[...] + lora_a)
    k2_f = k_f * (1.0 + (a_f - 1.0) * ka_ref[...])
    lw_hi, lw_lo = _split2(logw)
    cum = _dot(lower2, jnp.concatenate([lw_hi, lw_lo], axis=0))
    p_in = jnp.exp(cum)
    q_in = jnp.exp(-cum)

    r = slabs(r_f)
    v = slabs(v_f)
    k2 = slabs(k2_f)
    a = slabs(a_f)
    kkr = slabs(k_f * kk_ref[...])
    kk = kkr * lax.rsqrt(head_sum(kkr * kkr) + L2_EPS)
    q3 = slabs(q_in)
    at = kk * slabs(jnp.exp(cum - logw))
    kt = k2 * q3
    bt = a * kk * q3
    rt = slabs(r_f * p_in)
    p_last = slabs(p_in[L - 1:L, :])

    a_s, k_s, b_s, r_s, v_s = stack(at), stack(kt), stack(bt), stack(rt), stack(v)
    kh_s, bh_s = stack(kt * p_last), stack(bt * p_last)

    prod = _bdot_nt(jnp.concatenate([a_s, r_s], axis=1).astype(BF16),
                    jnp.concatenate([k_s, b_s], axis=1).astype(BF16))
    akk = jnp.where(strict_m, prod[:, 0:2 * L, 0:2 * L], 0.0)
    aab = jnp.where(strict_m, prod[:, 0:2 * L, 2 * L:4 * L], 0.0)
    ark = jnp.where(incl_m, prod[:, 2 * L:4 * L, 0:2 * L], 0.0)
    arb = jnp.where(incl_m, prod[:, 2 * L:4 * L, 2 * L:4 * L], 0.0)

    tinv = eye_f - aab
    pw = bmm(aab, aab)
    n_sq = 2
    while 2 * n_sq < L:
        both = bmm(jnp.concatenate([tinv, pw], axis=1), pw)
        tinv = tinv + both[:, 0:2 * L]
        pw = both[:, 2 * L:4 * L]
        n_sq *= 2
    tinv = tinv + bmm(tinv, pw)

    akv = bmm(jnp.concatenate([akk, ark], axis=1), v_s)
    akk_v, ark_v = akv[:, 0:2 * L], akv[:, 2 * L:4 * L]
    au = bmm(tinv, jnp.concatenate([a_s, akk_v], axis=2))
    corr = bmm(arb, au)
    q_s = r_s - corr[:, :, 0:LANES]
    y0 = ark_v - corr[:, :, LANES:2 * LANES]
    gh_rhs = jnp.concatenate(
        [-au, jnp.concatenate([jnp.zeros_like(v_s), v_s], axis=2)], axis=1)
    gh = _bdot_tn(jnp.concatenate([bh_s, kh_s], axis=1).astype(BF16), gh_rhs.astype(BF16))
    g_mat = jnp.where(eye, p_last, 0.0) + gh[:, :, 0:LANES]
    h_mat = gh[:, :, LANES:2 * LANES]

    state = st_ref[...]
    upd = bmm(jnp.concatenate([q_s, g_mat], axis=1), state)
    st_ref[...] = upd[:, 2 * L:4 * L] + h_mat
    y_s = upd[:, 0:2 * L] + y0
    y = y_s[:, 0:L] + y_s[:, L:2 * L]

    mean = head_sum(y) * (1.0 / HEAD_DIM)
    yc = y - mean
    var = head_sum(yc * yc) * (1.0 / HEAD_DIM)
    yn = yc * lax.rsqrt(var + LNX_EPS) * slabs(lw_ref[...]) + slabs(lb_ref[...])
    bonus = head_sum(r * k2 * slabs(rk_ref[...])) * v
    g = slabs(g_ref[0])
    out = ((yn + bonus) * (g * _sigmoid(g))).astype(o_ref.dtype)
    for s in range(nb):
        o_ref[0, :, s * LANES:(s + 1) * LANES] = out[s]


def _rwkv(u1, u2, mu, w0, wup_p, a0, aup_p, k_k, k_a, r_k, lnx_w, lnx_b, width):
    b, s, sw = u1.shape
    n_slab = width // LANES
    row = lambda a: a.reshape(1, -1)
    vec = pl.BlockSpec((1, width), lambda i, j: (0, 0))
    return pl.pallas_call(
        functools.partial(_rwkv_kernel, width=width),
        grid=(b, s // CHUNK),
        in_specs=[
            pl.BlockSpec((1, CHUNK, sw), lambda i, j: (i, j, 0)),
            pl.BlockSpec((1, CHUNK, width), lambda i, j: (i, j, 0)),
            pl.BlockSpec((1, sw), lambda i, j: (0, 0)),
            vec,
            pl.BlockSpec((LANES, width), lambda i, j: (0, 0)),
            vec,
            pl.BlockSpec((LANES, width), lambda i, j: (0, 0)),
            vec, vec, vec, vec, vec,
        ],
        out_specs=pl.BlockSpec((1, CHUNK, width), lambda i, j: (i, j, 0)),
        out_shape=jax.ShapeDtypeStruct((b, s, width), BF16),
        scratch_shapes=[pltpu.VMEM((8, sw), F32),
                        pltpu.VMEM((n_slab, LANES, LANES), F32)],
        compiler_params=pltpu.CompilerParams(
            dimension_semantics=("parallel", "arbitrary"),
            vmem_limit_bytes=VMEM_LIMIT),
        name="rwkv7_chunked",
    )(u1, u2, row(mu), row(w0), wup_p, row(a0), aup_p, row(k_k), row(k_a), row(r_k),
      row(lnx_w), row(lnx_b))


def _stickbreak_kernel(q_ref, k_ref, v_ref, o_ref, vt_ref):
    i = pl.program_id(2)
    seq = k_ref.shape[1]
    heads = range(LANES // HEAD_DIM)
    blocks = range(SB_KG)

    @pl.when(i == 0)
    def _():
        def transpose_chunk(c, _):
            c0 = pl.multiple_of(c * SB_TQ, SB_TQ)
            vt_ref[:, pl.ds(c0, SB_TQ)] = v_ref[0, pl.ds(c0, SB_TQ), :].astype(F32).T.astype(BF16)
            return 0
        lax.fori_loop(0, seq // SB_TQ, transpose_chunk, 0)

    qt = q_ref[0].astype(F32).T
    hrow = lax.broadcasted_iota(jnp.int32, qt.shape, 0)
    qs = [jnp.where((hrow >= h * HEAD_DIM) & (hrow < (h + 1) * HEAD_DIM), qt, 0.0).astype(BF16)
          for h in heads]
    rows = lax.broadcasted_iota(jnp.int32, (SB_TK, SB_TQ), 0)
    cols = lax.broadcasted_iota(jnp.int32, (SB_TK, SB_TQ), 1)
    ur = lax.broadcasted_iota(jnp.int32, (SB_TK, SB_TK), 0)
    uc = lax.broadcasted_iota(jnp.int32, (SB_TK, SB_TK), 1)
    upper = jnp.where(uc > ur, 1.0, 0.0).astype(BF16)

    def group(kbase, state, masked):
        carries, accs = list(state[:2]), list(state[2:])
        k0s = [pl.multiple_of(kbase + m * SB_TK, SB_TK) for m in blocks]
        kbs = [k_ref[0, pl.ds(k0s[m], SB_TK), :] for m in blocks]
        zs = [[_dot(kbs[m], qs[h]) for m in blocks] for h in heads]
        stricts = [(k0s[m] + rows) < (i * SB_TQ + cols) for m in blocks] if masked else None
        log_betas = [[None] * SB_KG for _ in heads]
        log_1mbs = [[None] * SB_KG for _ in heads]
        for h in heads:
            for m in blocks:
                z = zs[h][m]
                soft = jnp.log2(1.0 + jnp.exp2(-jnp.abs(z)))
                log_beta = jnp.minimum(z, 0.0) - soft
                log_1mb = log_beta - z
                if masked:
                    log_1mb = jnp.where(stricts[m], log_1mb, 0.0)
                log_betas[h][m] = log_beta
                log_1mbs[h][m] = log_1mb
        suffixes = [[_dot(upper, log_1mbs[h][m].astype(BF16)) for m in blocks] for h in heads]
        prs = [[None] * SB_KG for _ in heads]
        tots = [[None] * SB_KG for _ in heads]
        for h in heads:
            for m in blocks:
                tots[h][m] = suffixes[h][m][0:1, :] + log_1mbs[h][m][0:1, :]
                pr = jnp.exp2(log_betas[h][m] + suffixes[h][m])
                if masked:
                    pr = jnp.where(stricts[m], pr, 0.0)
                prs[h][m] = pr.astype(BF16)
        pvs = [[_dot(vt_ref[h * HEAD_DIM:(h + 1) * HEAD_DIM, pl.ds(k0s[m], SB_TK)], prs[h][m])
                for m in blocks] for h in heads]
        for h in heads:
            for m in reversed(blocks):
                accs[h] = accs[h] + jnp.exp2(carries[h]) * pvs[h][m]
                carries[h] = carries[h] + tots[h][m]
        return tuple(carries) + tuple(accs)

    zero_c = jnp.zeros((1, SB_TQ), F32)
    zero_a = jnp.zeros((HEAD_DIM, SB_TQ), F32)
    state = group(i * SB_TQ, (zero_c, zero_c, zero_a, zero_a), True)

    def more(jj_state):
        jj, st = jj_state
        alive = jnp.max(jnp.maximum(st[0], st[1])) > SB_DEAD_LOG2
        return jnp.logical_and(jj < i, alive)

    def sweep(jj_state):
        jj, st = jj_state
        return jj + 1, group((i - 1 - jj) * SB_TQ, st, False)

    _, state = lax.while_loop(more, sweep, (jnp.int32(0), state))
    o_ref[0] = jnp.concatenate(state[2:], axis=0).T


def _stickbreak(qkv, width):
    b, s, _ = qkv.shape
    n_pair = width // LANES
    return pl.pallas_call(
        _stickbreak_kernel,
        grid=(b, n_pair, s // SB_TQ),
        in_specs=[
            pl.BlockSpec((1, SB_TQ, LANES), lambda bi, pi, qi: (bi, qi, pi)),
            pl.BlockSpec((1, s, LANES), lambda bi, pi, qi: (bi, 0, n_pair + pi)),
            pl.BlockSpec((1, s, LANES), lambda bi, pi, qi: (bi, 0, 2 * n_pair + pi)),
        ],
        out_specs=pl.BlockSpec((1, SB_TQ, LANES), lambda bi, pi, qi: (bi, qi, pi)),
        out_shape=jax.ShapeDtypeStruct((b, s, width), F32),
        scratch_shapes=[pltpu.VMEM((LANES, s), BF16)],
        compiler_params=pltpu.CompilerParams(
            dimension_semantics=("arbitrary", "arbitrary", "arbitrary"),
            vmem_limit_bytes=VMEM_LIMIT),
        name="stickbreak_attn",
    )(qkv, qkv, qkv)


def _final_kernel(x_ref, ya_ref, yb_ref, gsb_ref, p_ref, wout_ref, gpost_ref, wple_ref,
                  gple_ref, wplg_ref, o_ref, *, width):
    def rms(t, gain):
        return t * lax.rsqrt(jnp.mean(t * t, axis=-1, keepdims=True) + RMS_EPS) * gain

    gs = gsb_ref[...]
    yb = (yb_ref[...] * (gs * _sigmoid(gs))).astype(BF16)
    mix = _dot(ya_ref[...], wout_ref[0:width, :]) + _dot(yb, wout_ref[width:, :])
    h = x_ref[...] + rms(mix, gpost_ref[...])
    ple = rms(_dot(p_ref[...].astype(BF16), wple_ref[...]), gple_ref[...])
    gate = _sigmoid(_dot(h.astype(BF16), wplg_ref[...]))
    o_ref[...] = h + ple * gate


def _final(x2, ya, yb, u2, p2, w_out, g_post, w_ple, g_ple, w_plg, width, gsb_block, tm):
    m, d = x2.shape
    pd = p2.shape[1]
    const = lambda shape: pl.BlockSpec(shape, lambda i: (0, 0), pipeline_mode=pl.Buffered(1))
    return pl.pallas_call(
        functools.partial(_final_kernel, width=width),
        grid=(m // tm,),
        in_specs=[
            pl.BlockSpec((tm, d), lambda i: (i, 0)),
            pl.BlockSpec((tm, width), lambda i: (i, 0)),
            pl.BlockSpec((tm, width), lambda i: (i, 0)),
            pl.BlockSpec((tm, width), lambda i: (i, gsb_block)),
            pl.BlockSpec((tm, pd), lambda i: (i, 0)),
            const((2 * width, d)),
            const((1, d)),
            const((pd, d)),
            const((1, d)),
            const((d, d)),
        ],
        out_specs=pl.BlockSpec((tm, d), lambda i: (i, 0)),
        out_shape=jax.ShapeDtypeStruct((m, d), F32),
        compiler_params=pltpu.CompilerParams(
            dimension_semantics=("parallel",),
            vmem_limit_bytes=VMEM_LIMIT),
        name="out_proj_ple",
    )(x2, ya, yb, u2, p2, w_out, g_post.reshape(1, d), w_ple, g_ple.reshape(1, d), w_plg)


def kernel(x, p, g_pre, w_in, mu_shift, w0, w_up, a0, a_up, k_k, k_a, r_k, lnx_w, lnx_b,
           w_out, g_post, w_ple, w_plg, g_ple):
    bsz, seq, d = x.shape
    depth = w_in.shape[0]
    width = w0.shape[1]
    sw = mu_shift.shape[1]
    lora = w_up.shape[1]
    m = bsz * seq
    h2 = x.reshape(m, d)
    for i in range(depth):
        w_in_b = w_in[i].astype(BF16)
        u1 = _norm_matmul(h2, g_pre[i], w_in_b[:, :sw], tm=512)
        gates, qkv = _norm_matmul_split(h2, g_pre[i], w_in_b[:, sw:],
                                        LOG2E * HEAD_DIM ** -0.5, tm=1024, tn=width)

        zpad = jnp.zeros((LANES - lora, width), F32)
        wup_p = jnp.concatenate([w_up[i], zpad], axis=0).astype(BF16)
        aup_p = jnp.concatenate([zpad, a_up[i]], axis=0).astype(BF16)
        ya = _rwkv(u1.reshape(bsz, seq, sw), gates.reshape(bsz, seq, 2 * width), mu_shift[i],
                   w0[i], wup_p, a0[i], aup_p, k_k[i], k_a[i], r_k[i], lnx_w[i], lnx_b[i],
                   width)
        yb = _stickbreak(qkv.reshape(bsz, seq, 3 * width), width)
        h2 = _final(h2, ya.reshape(m, width), yb.reshape(m, width), gates, p[i].reshape(m, -1),
                    w_out[i].astype(BF16), g_post[i], w_ple[i].astype(BF16), g_ple[i],
                    w_plg[i].astype(BF16), width, gsb_block=1, tm=512)
    return h2.reshape(bsz, seq, d)
```

```python
import functools

import jax
import jax.numpy as jnp
from jax import lax
from jax.experimental import pallas as pl
from jax.experimental.pallas import tpu as pltpu

F32 = jnp.float32
BF16 = jnp.bfloat16

HEAD_DIM = 64
LANES = 128
RMS_EPS = 1e-6
LNX_EPS = 64e-5
L2_EPS = 1e-12
CHUNK = 64
SB_TK = 128
SB_KG = 4
SB_SWEEP_KG = 2
SB_TQ = SB_KG * SB_TK
LOG2E = 1.4426950408889634
SB_DEAD_LOG2 = -160.0
VMEM_LIMIT = 56 * 1024 * 1024


def _dot(a, b):
    return jnp.dot(a, b, preferred_element_type=F32)


def _bdot(a, b):
    return lax.dot_general(a, b, (((2,), (1,)), ((0,), (0,))), preferred_element_type=F32)


def _bdot_nt(a, b):
    return lax.dot_general(a, b, (((2,), (2,)), ((0,), (0,))), preferred_element_type=F32)


def _bdot_tn(a, b):
    return lax.dot_general(a, b, (((1,), (1,)), ((0,), (0,))), preferred_element_type=F32)


def _split2(x):
    hi = x.astype(BF16)
    lo = (x - hi.astype(F32)).astype(BF16)
    return hi, lo


def _sigmoid(x):
    return 1.0 / (1.0 + jnp.exp(-x))


def _rms_bf16(x, gain):
    ms = jnp.mean(x * x, axis=-1, keepdims=True)
    return (x * lax.rsqrt(ms + RMS_EPS) * gain).astype(BF16)


def _norm_matmul_kernel(x_ref, g_ref, w_ref, o_ref):
    o_ref[...] = _dot(_rms_bf16(x_ref[...], g_ref[...]), w_ref[...])


def _norm_matmul(x2, gain, w, tm):
    m, k = x2.shape
    n = w.shape[1]
    return pl.pallas_call(
        _norm_matmul_kernel,
        grid=(m // tm,),
        in_specs=[
            pl.BlockSpec((tm, k), lambda i: (i, 0)),
            pl.BlockSpec((1, k), lambda i: (0, 0)),
            pl.BlockSpec((k, n), lambda i: (0, 0), pipeline_mode=pl.Buffered(1)),
        ],
        out_specs=pl.BlockSpec((tm, n), lambda i: (i, 0)),
        out_shape=jax.ShapeDtypeStruct((m, n), F32),
        compiler_params=pltpu.CompilerParams(
            dimension_semantics=("parallel",),
            vmem_limit_bytes=VMEM_LIMIT),
        name="norm_matmul_shift",
    )(x2, gain.reshape(1, k), w)


def _norm_matmul_split_kernel(x_ref, g_ref, w_ref, gate_ref, qkv_ref, xn_ref, *, q_scale):
    j = pl.program_id(1)
    last = pl.num_programs(1) - 1

    @pl.when(j == 0)
    def _():
        xn_ref[...] = _rms_bf16(x_ref[...], g_ref[...])

    acc = _dot(xn_ref[...], w_ref[...])
    is_gate = jnp.logical_or(j == 0, j == last)

    @pl.when(is_gate)
    def _():
        gate_ref[...] = acc

    @pl.when(j == 1)
    def _():
        qkv_ref[...] = (acc * q_scale).astype(BF16)

    @pl.when(jnp.logical_and(j > 1, j < last))
    def _():
        qkv_ref[...] = acc.astype(BF16)


def _norm_matmul_split(x2, gain, w, q_scale, tm, tn):
    m, k = x2.shape
    n_tiles = w.shape[1] // tn
    assert n_tiles == 5
    return pl.pallas_call(
        functools.partial(_norm_matmul_split_kernel, q_scale=q_scale),
        grid=(m // tm, n_tiles),
        in_specs=[
            pl.BlockSpec((tm, k), lambda i, j: (i, 0)),
            pl.BlockSpec((1, k), lambda i, j: (0, 0)),
            pl.BlockSpec((k, tn), lambda i, j: (0, j)),
        ],
        out_specs=[
            pl.BlockSpec((tm, tn), lambda i, j: (i, j // (n_tiles - 1))),
            pl.BlockSpec((tm, tn), lambda i, j: (i, jnp.clip(j - 1, 0, n_tiles - 3))),
        ],
        out_shape=[jax.ShapeDtypeStruct((m, 2 * tn), F32),
                   jax.ShapeDtypeStruct((m, 3 * tn), BF16)],
        scratch_shapes=[pltpu.VMEM((tm, k), BF16)],
        compiler_params=pltpu.CompilerParams(
            dimension_semantics=("parallel", "arbitrary"),
            vmem_limit_bytes=VMEM_LIMIT),
        name="norm_matmul_split",
    )(x2, gain.reshape(1, k), w)


def _rwkv_kernel(u_ref, g_ref, mu_ref, w0_ref, wup_ref, a0_ref, aup_ref, kk_ref, ka_ref,
                 rk_ref, lw_ref, lb_ref, o_ref, prev_ref, st_ref, *, width):
    L = CHUNK
    nb = width // LANES

    @pl.when(pl.program_id(1) == 0)
    def _():
        prev_ref[...] = jnp.zeros_like(prev_ref)
        st_ref[...] = jnp.zeros_like(st_ref)

    u = u_ref[0]
    row = lax.broadcasted_iota(jnp.int32, u.shape, 0)
    u_prev = jnp.where(row == 0, prev_ref[0:1, :], pltpu.roll(u, 1, 0))
    prev_ref[0:1, :] = u[L - 1:L, :]
    us = u + (u_prev - u) * mu_ref[...]

    lane = lax.broadcasted_iota(jnp.int32, (L, LANES), 1)
    m0 = lane < HEAD_DIM
    xl = us[:, 3 * width:3 * width + LANES]
    xl = jnp.where(m0, jnp.tanh(xl), xl).astype(BF16)
    lora_w = _dot(xl, wup_ref[...])
    lora_a = _dot(xl, aup_ref[...])

    r2 = lax.broadcasted_iota(jnp.int32, (2 * L, 2 * L), 0)
    c2 = lax.broadcasted_iota(jnp.int32, (2 * L, 2 * L), 1)
    same = (r2 >= L) == (c2 >= L)
    strict_m = same & (r2 > c2)
    incl_m = same & (r2 >= c2)
    eye = r2 == c2
    eye_f = jnp.where(eye, 1.0, 0.0).astype(F32)
    bo = jnp.where(same, 1.0, 0.0).astype(BF16)
    block_ones2 = jnp.concatenate([bo, bo], axis=0)
    rl = lax.broadcasted_iota(jnp.int32, (L, 2 * L), 0)
    cl = lax.broadcasted_iota(jnp.int32, (L, 2 * L), 1)
    lower2 = jnp.where(rl >= (cl & (L - 1)), 1.0, 0.0).astype(BF16)

    def slabs(x):
        return jnp.stack([x[:, s * LANES:(s + 1) * LANES] for s in range(nb)], axis=0)

    def head_sum(x):
        hi, lo = _split2(x.reshape(nb * L, LANES))
        return _dot(jnp.concatenate([hi, lo], axis=1), block_ones2).reshape(nb, L, LANES)

    def stack(x):
        return jnp.concatenate([jnp.where(m0, x, 0.0), jnp.where(m0, 0.0, x)], axis=1)

    def bmm(a, b):
        return _bdot(a.astype(BF16), b.astype(BF16))

    r_f = us[:, 0:width]
    k_f = us[:, width:2 * width]
    v_f = us[:, 2 * width:3 * width]
    wpre = w0_ref[...] + lora_w
    softplus = jnp.maximum(-wpre, 0.0) + jnp.log(1.0 + jnp.exp(-jnp.abs(wpre)))
    logw = -jnp.exp(-softplus - 0.5)
    a_f = _sigmoid(a0_ref[...] + lora_a)
    k2_f = k_f * (1.0 + (a_f - 1.0) * ka_ref[...])
    lw_hi, lw_lo = _split2(logw)
    cum = _dot(lower2, jnp.concatenate([lw_hi, lw_lo], axis=0))
    p_in = jnp.exp(cum)
    q_in = jnp.exp(-cum)

    r = slabs(r_f)
    v = slabs(v_f)
    k2 = slabs(k2_f)
    a = slabs(a_f)
    kkr = slabs(k_f * kk_ref[...])
    kk = kkr * lax.rsqrt(head_sum(kkr * kkr) + L2_EPS)
    q3 = slabs(q_in)
    at = kk * slabs(jnp.exp(cum - logw))
    kt = k2 * q3
    bt = a * kk * q3
    rt = slabs(r_f * p_in)
    p_last = slabs(p_in[L - 1:L, :])

    a_s, k_s, b_s, r_s, v_s = stack(at), stack(kt), stack(bt), stack(rt), stack(v)
    kh_s, bh_s = stack(kt * p_last), stack(bt * p_last)

    prod = _bdot_nt(jnp.concatenate([a_s, r_s], axis=1).astype(BF16),
                    jnp.concatenate([k_s, b_s], axis=1).astype(BF16))
    akk = jnp.where(strict_m, prod[:, 0:2 * L, 0:2 * L], 0.0)
    aab = jnp.where(strict_m, prod[:, 0:2 * L, 2 * L:4 * L], 0.0)
    ark = jnp.where(incl_m, prod[:, 2 * L:4 * L, 0:2 * L], 0.0)
    arb = jnp.where(incl_m, prod[:, 2 * L:4 * L, 2 * L:4 * L], 0.0)

    tinv = eye_f - aab
    pw = bmm(aab, aab)
    n_sq = 2
    while 2 * n_sq < L:
        both = bmm(jnp.concatenate([tinv, pw], axis=1), pw)
        tinv = tinv + both[:, 0:2 * L]
        pw = both[:, 2 * L:4 * L]
        n_sq *= 2
    tinv = tinv + bmm(tinv, pw)

    akv = bmm(jnp.concatenate([akk, ark], axis=1), v_s)
    akk_v, ark_v = akv[:, 0:2 * L], akv[:, 2 * L:4 * L]
    au = bmm(tinv, jnp.concatenate([a_s, akk_v], axis=2))
    corr = bmm(arb, au)
    q_s = r_s - corr[:, :, 0:LANES]
    y0 = ark_v - corr[:, :, LANES:2 * LANES]
    gh_rhs = jnp.concatenate(
        [-au, jnp.concatenate([jnp.zeros_like(v_s), v_s], axis=2)], axis=1)
    gh = _bdot_tn(jnp.concatenate([bh_s, kh_s], axis=1).astype(BF16), gh_rhs.astype(BF16))
    g_mat = jnp.where(eye, p_last, 0.0) + gh[:, :, 0:LANES]
    h_mat = gh[:, :, LANES:2 * LANES]

    state = st_ref[...]
    upd = bmm(jnp.concatenate([q_s, g_mat], axis=1), state)
    st_ref[...] = upd[:, 2 * L:4 * L] + h_mat
    y_s = upd[:, 0:2 * L] + y0
    y = y_s[:, 0:L] + y_s[:, L:2 * L]

    mean = head_sum(y) * (1.0 / HEAD_DIM)
    yc = y - mean
    var = head_sum(yc * yc) * (1.0 / HEAD_DIM)
    yn = yc * lax.rsqrt(var + LNX_EPS) * slabs(lw_ref[...]) + slabs(lb_ref[...])
    bonus = head_sum(r * k2 * slabs(rk_ref[...])) * v
    g = slabs(g_ref[0])
    out = ((yn + bonus) * (g * _sigmoid(g))).astype(o_ref.dtype)
    for s in range(nb):
        o_ref[0, :, s * LANES:(s + 1) * LANES] = out[s]


def _rwkv(u1, u2, mu, w0, wup_p, a0, aup_p, k_k, k_a, r_k, lnx_w, lnx_b, width):
    b, s, sw = u1.shape
    n_slab = width // LANES
    row = lambda a: a.reshape(1, -1)
    vec = pl.BlockSpec((1, width), lambda i, j: (0, 0))
    return pl.pallas_call(
        functools.partial(_rwkv_kernel, width=width),
        grid=(b, s // CHUNK),
        in_specs=[
            pl.BlockSpec((1, CHUNK, sw), lambda i, j: (i, j, 0)),
            pl.BlockSpec((1, CHUNK, width), lambda i, j: (i, j, 0)),
            pl.BlockSpec((1, sw), lambda i, j: (0, 0)),
            vec,
            pl.BlockSpec((LANES, width), lambda i, j: (0, 0)),
            vec,
            pl.BlockSpec((LANES, width), lambda i, j: (0, 0)),
            vec, vec, vec, vec, vec,
        ],
        out_specs=pl.BlockSpec((1, CHUNK, width), lambda i, j: (i, j, 0)),
        out_shape=jax.ShapeDtypeStruct((b, s, width), BF16),
        scratch_shapes=[pltpu.VMEM((8, sw), F32),
                        pltpu.VMEM((n_slab, LANES, LANES), F32)],
        compiler_params=pltpu.CompilerParams(
            dimension_semantics=("parallel", "arbitrary"),
            vmem_limit_bytes=VMEM_LIMIT),
        name="rwkv7_chunked",
    )(u1, u2, row(mu), row(w0), wup_p, row(a0), aup_p, row(k_k), row(k_a), row(r_k),
      row(lnx_w), row(lnx_b))


def _stickbreak_kernel(q_ref, k_ref, v_ref, o_ref, vt_ref):
    i = pl.program_id(2)
    seq = k_ref.shape[1]
    heads = range(LANES // HEAD_DIM)

    @pl.when(i == 0)
    def _():
        def transpose_chunk(c, _):
            c0 = pl.multiple_of(c * SB_TQ, SB_TQ)
            vt_ref[:, pl.ds(c0, SB_TQ)] = v_ref[0, pl.ds(c0, SB_TQ), :].astype(F32).T.astype(BF16)
            return 0
        lax.fori_loop(0, seq // SB_TQ, transpose_chunk, 0)

    qt = q_ref[0].astype(F32).T
    hrow = lax.broadcasted_iota(jnp.int32, qt.shape, 0)
    qs = [jnp.where((hrow >= h * HEAD_DIM) & (hrow < (h + 1) * HEAD_DIM), qt, 0.0).astype(BF16)
          for h in heads]
    ur = lax.broadcasted_iota(jnp.int32, (SB_TK, SB_TK), 0)
    uc = lax.broadcasted_iota(jnp.int32, (SB_TK, SB_TK), 1)
    upper = jnp.where(uc > ur, 1.0, 0.0).astype(BF16)

    def group(kbase, state, masked, n_blocks):
        blocks = range(n_blocks)
        carries, accs = list(state[:2]), list(state[2:])
        k0s = [pl.multiple_of(kbase + m * SB_TK, SB_TK) for m in blocks]
        kbs = [k_ref[0, pl.ds(k0s[m], SB_TK), :] for m in blocks]
        c0s = [m * SB_TK if masked else 0 for m in blocks]
        zs = [[_dot(kbs[m], qs[h][:, c0s[m]:]) for m in blocks] for h in heads]
        stricts = [(k0s[m] + lax.broadcasted_iota(jnp.int32, (SB_TK, SB_TQ - c0s[m]), 0))
                   < (i * SB_TQ + c0s[m]
                      + lax.broadcasted_iota(jnp.int32, (SB_TK, SB_TQ - c0s[m]), 1))
                   for m in blocks] if masked else None

        def full_width(x, m):
            if c0s[m] == 0:
                return x
            return jnp.concatenate([jnp.zeros((x.shape[0], c0s[m]), x.dtype), x], axis=1)
        log_betas = [[None] * n_blocks for _ in heads]
        log_1mbs = [[None] * n_blocks for _ in heads]
        for h in heads:
            for m in blocks:
                z = zs[h][m]
                soft = jnp.log2(1.0 + jnp.exp2(-jnp.abs(z)))
                log_beta = jnp.minimum(z, 0.0) - soft
                log_1mb = log_beta - z
                if masked:
                    log_1mb = jnp.where(stricts[m], log_1mb, 0.0)
                log_betas[h][m] = log_beta
                log_1mbs[h][m] = log_1mb
        suffixes = [[_dot(upper, log_1mbs[h][m].astype(BF16)) for m in blocks] for h in heads]
        prs = [[None] * n_blocks for _ in heads]
        tots = [[None] * n_blocks for _ in heads]
        for h in heads:
            for m in blocks:
                tot8 = suffixes[h][m][0:8, :] + log_1mbs[h][m][0:8, :]
                tots[h][m] = full_width(tot8, m)[0:1, :]
                pr = jnp.exp2(log_betas[h][m] + suffixes[h][m])
                if masked:
                    pr = jnp.where(stricts[m], pr, 0.0)
                prs[h][m] = pr.astype(BF16)
        pvs = [[_dot(vt_ref[h * HEAD_DIM:(h + 1) * HEAD_DIM, pl.ds(k0s[m], SB_TK)], prs[h][m])
                for m in blocks] for h in heads]
        for h in heads:
            for m in reversed(blocks):
                accs[h] = accs[h] + jnp.exp2(carries[h]) * full_width(pvs[h][m], m)
                carries[h] = carries[h] + tots[h][m]
        return tuple(carries) + tuple(accs)

    zero_c = jnp.zeros((1, SB_TQ), F32)
    zero_a = jnp.zeros((HEAD_DIM, SB_TQ), F32)
    state = group(i * SB_TQ, (zero_c, zero_c, zero_a, zero_a), True, SB_KG)

    def more(jj_state):
        jj, st = jj_state
        alive = jnp.max(jnp.maximum(st[0], st[1])) > SB_DEAD_LOG2
        return jnp.logical_and(jj < i * (SB_KG // SB_SWEEP_KG), alive)

    def sweep(jj_state):
        jj, st = jj_state
        kbase = i * SB_TQ - (jj + 1) * (SB_SWEEP_KG * SB_TK)
        return jj + 1, group(kbase, st, False, SB_SWEEP_KG)

    _, state = lax.while_loop(more, sweep, (jnp.int32(0), state))
    o_ref[0] = jnp.concatenate(state[2:], axis=0).T


def _stickbreak(qkv, width):
    b, s, _ = qkv.shape
    n_pair = width // LANES
    return pl.pallas_call(
        _stickbreak_kernel,
        grid=(b, n_pair, s // SB_TQ),
        in_specs=[
            pl.BlockSpec((1, SB_TQ, LANES), lambda bi, pi, qi: (bi, qi, pi)),
            pl.BlockSpec((1, s, LANES), lambda bi, pi, qi: (bi, 0, n_pair + pi)),
            pl.BlockSpec((1, s, LANES), lambda bi, pi, qi: (bi, 0, 2 * n_pair + pi)),
        ],
        out_specs=pl.BlockSpec((1, SB_TQ, LANES), lambda bi, pi, qi: (bi, qi, pi)),
        out_shape=jax.ShapeDtypeStruct((b, s, width), F32),
        scratch_shapes=[pltpu.VMEM((LANES, s), BF16)],
        compiler_params=pltpu.CompilerParams(
            dimension_semantics=("arbitrary", "arbitrary", "arbitrary"),
            vmem_limit_bytes=VMEM_LIMIT),
        name="stickbreak_attn",
    )(qkv, qkv, qkv)


def _final_kernel(x_ref, ya_ref, yb_ref, gsb_ref, p_ref, wout_ref, gpost_ref, wple_ref,
                  gple_ref, wplg_ref, o_ref, *, width):
    def rms(t, gain):
        return t * lax.rsqrt(jnp.mean(t * t, axis=-1, keepdims=True) + RMS_EPS) * gain

    gs = gsb_ref[...]
    yb = (yb_ref[...] * (gs * _sigmoid(gs))).astype(BF16)
    mix = _dot(ya_ref[...], wout_ref[0:width, :]) + _dot(yb, wout_ref[width:, :])
    h = x_ref[...] + rms(mix, gpost_ref[...])
    ple = rms(_dot(p_ref[...].astype(BF16), wple_ref[...]), gple_ref[...])
    gate = _sigmoid(_dot(h.astype(BF16), wplg_ref[...]))
    o_ref[...] = h + ple * gate


def _final(x2, ya, yb, u2, p2, w_out, g_post, w_ple, g_ple, w_plg, width, gsb_block, tm):
    m, d = x2.shape
    pd = p2.shape[1]
    const = lambda shape: pl.BlockSpec(shape, lambda i: (0, 0), pipeline_mode=pl.Buffered(1))
    return pl.pallas_call(
        functools.partial(_final_kernel, width=width),
        grid=(m // tm,),
        in_specs=[
            pl.BlockSpec((tm, d), lambda i: (i, 0)),
            pl.BlockSpec((tm, width), lambda i: (i, 0)),
            pl.BlockSpec((tm, width), lambda i: (i, 0)),
            pl.BlockSpec((tm, width), lambda i: (i, gsb_block)),
            pl.BlockSpec((tm, pd), lambda i: (i, 0)),
            const((2 * width, d)),
            const((1, d)),
            const((pd, d)),
            const((1, d)),
            const((d, d)),
        ],
        out_specs=pl.BlockSpec((tm, d), lambda i: (i, 0)),
        out_shape=jax.ShapeDtypeStruct((m, d), F32),
        compiler_params=pltpu.CompilerParams(
            dimension_semantics=("parallel",),
            vmem_limit_bytes=VMEM_LIMIT),
        name="out_proj_ple",
    )(x2, ya, yb, u2, p2, w_out, g_post.reshape(1, d), w_ple, g_ple.reshape(1, d), w_plg)


def kernel(x, p, g_pre, w_in, mu_shift, w0, w_up, a0, a_up, k_k, k_a, r_k, lnx_w, lnx_b,
           w_out, g_post, w_ple, w_plg, g_ple):
    bsz, seq, d = x.shape
    depth = w_in.shape[0]
    width = w0.shape[1]
    sw = mu_shift.shape[1]
    lora = w_up.shape[1]
    m = bsz * seq
    h2 = x.reshape(m, d)
    for i in range(depth):
        w_in_b = w_in[i].astype(BF16)
        u1 = _norm_matmul(h2, g_pre[i], w_in_b[:, :sw], tm=512)
        gates, qkv = _norm_matmul_split(h2, g_pre[i], w_in_b[:, sw:],
                                        LOG2E * HEAD_DIM ** -0.5, tm=1024, tn=width)

        zpad = jnp.zeros((LANES - lora, width), F32)
        wup_p = jnp.concatenate([w_up[i], zpad], axis=0).astype(BF16)
        aup_p = jnp.concatenate([zpad, a_up[i]], axis=0).astype(BF16)
        ya = _rwkv(u1.reshape(bsz, seq, sw), gates.reshape(bsz, seq, 2 * width), mu_shift[i],
                   w0[i], wup_p, a0[i], aup_p, k_k[i], k_a[i], r_k[i], lnx_w[i], lnx_b[i],
                   width)
        yb = _stickbreak(qkv.reshape(bsz, seq, 3 * width), width)
        h2 = _final(h2, ya.reshape(m, width), yb.reshape(m, width), gates, p[i].reshape(m, -1),
                    w_out[i].astype(BF16), g_post[i], w_ple[i].astype(BF16), g_ple[i],
                    w_plg[i].astype(BF16), width, gsb_block=1, tm=512)
    return h2.reshape(bsz, seq, d)
```
